```python
import math
import jax, jax.numpy as jnp
from jax import lax
import numpy as np

D_MODEL = 1024
BATCH = 4
SEQ = 8192
DEPTH = 2
DEC_BATCH = 128
DEC_SEQ = 1
PAST_LEN = 16384
PAGE_SIZE = 128

HEAD_DIM = 64
CONV_WIDTH = 256
CONV_K = 3
MLA_HEADS = 8
Q_RANK = 256
KV_RANK = 128
QK_NOPE = 64
QK_ROPE = 32
V_DIM = 64
ROPE_THETA = 10000.0
Q_BLOCK = 128
MLSTM_HEADS = 4
MLSTM_DH = 64
MLSTM_WIDTH = MLSTM_HEADS * MLSTM_DH
MLSTM_CHUNK = 64
MIX_WIDTH = CONV_WIDTH + MLA_HEADS * V_DIM + MLSTM_WIDTH
N_MIX_HEADS = MIX_WIDTH // HEAD_DIM
N_IN = 3 * CONV_WIDTH + Q_RANK + KV_RANK + QK_ROPE + 4 * MLSTM_WIDTH + 2 * MLSTM_HEADS
N_GROUPS = 4
EXPERTS_PER_GROUP = 8
N_EXPERTS = N_GROUPS * EXPERTS_PER_GROUP
TOP_K = 2
D_EXPERT = 256
MOE_BLOCK = 128
D_PLE = 256
DN_ALPHA = (2 * DEPTH) ** 0.25
DN_BETA = (8 * DEPTH) ** -0.25
LN_EPS = 1e-5
RMS_EPS = 1e-6

kernel_name = "hymba_conv_mla_mlstm_hmoe_step"


def layer_norm(x, g, b):
    xf = x.astype(jnp.float32)
    mu = jnp.mean(xf, -1, keepdims=True)
    var = jnp.mean(jnp.square(xf - mu), -1, keepdims=True)
    return ((xf - mu) * lax.rsqrt(var + LN_EPS) * g + b).astype(x.dtype)


def rms_norm(x, g):
    xf = x.astype(jnp.float32)
    return (xf * lax.rsqrt(jnp.mean(xf * xf, -1, keepdims=True) + RMS_EPS) * g).astype(x.dtype)


def rope(x, pos):
    half = QK_ROPE // 2
    inv = ROPE_THETA ** (-jnp.arange(half, dtype=jnp.float32) / half)
    ang = pos.astype(jnp.float32)[:, None] * inv
    ang = ang.reshape(ang.shape[:1] + (1,) * (x.ndim - 3) + ang.shape[1:])
    cos, sin = jnp.cos(ang), jnp.sin(ang)
    xf = x.astype(jnp.float32)
    x1, x2 = xf[..., :half], xf[..., half:]
    return jnp.concatenate([x1 * cos - x2 * sin, x1 * sin + x2 * cos], -1).astype(x.dtype)


def short_conv(u, buf, w):
    L = u.shape[1]
    full = jnp.concatenate([buf.astype(u.dtype), u], axis=1)
    y = full[:, 0:L] * w[0]
    for j in range(1, CONV_K):
        y = y + full[:, j:j + L] * w[j]
    return y, full[:, L:]


def mla_prompt_attention(q_nope, q_rope, k_nope, k_rope, v):
    B, S, H, _ = q_nope.shape
    scale = (QK_NOPE + QK_ROPE) ** -0.5
    kpos = jnp.arange(S)

    def block(i):
        qs = i * Q_BLOCK
        qn = lax.dynamic_slice_in_dim(q_nope, qs, Q_BLOCK, axis=1)
        qr = lax.dynamic_slice_in_dim(q_rope, qs, Q_BLOCK, axis=1)
        s = (jnp.einsum('bqhd,bkhd->bhqk', qn, k_nope, preferred_element_type=jnp.float32)
             + jnp.einsum('bqhd,bkd->bhqk', qr, k_rope, preferred_element_type=jnp.float32)) * scale
        qpos = qs + jnp.arange(Q_BLOCK)
        s = jnp.where(kpos[None, :] <= qpos[:, None], s, -jnp.inf)
        p = jax.nn.softmax(s, axis=-1)
        return jnp.einsum('bhqk,bkhd->bqhd', p.astype(v.dtype), v)

    out = lax.map(block, jnp.arange(S // Q_BLOCK))
    return jnp.moveaxis(out, 0, 1).reshape(B, S, H, V_DIM)


def mla_sample_attention(q_nope, q_rope, ckv_all, krope_all, w_uk, w_uv, past_len):
    S = q_nope.shape[1]
    scale = (QK_NOPE + QK_ROPE) ** -0.5
    q_lat = jnp.einsum('bshd,rhd->bshr', q_nope, w_uk)
    s = (jnp.einsum('bshr,bkr->bhsk', q_lat, ckv_all, preferred_element_type=jnp.float32)
         + jnp.einsum('bshd,bkd->bhsk', q_rope, krope_all, preferred_element_type=jnp.float32)) * scale
    kpos = jnp.arange(ckv_all.shape[1])
    qpos = past_len + jnp.arange(S)
    s = jnp.where(kpos[None, :] <= qpos[:, None], s, -jnp.inf)
    p = jax.nn.softmax(s, axis=-1)
    o_lat = jnp.einsum('bhsk,bkr->bshr', p.astype(ckv_all.dtype), ckv_all)
    return jnp.einsum('bshr,rhd->bshd', o_lat, w_uv)


def mlstm_chunkwise(q, k, v, ig, lf, C0, n0, m0):
    B, L, NH, DH = q.shape
    f32 = jnp.float32
    cl = MLSTM_CHUNK if L % MLSTM_CHUNK == 0 else L
    nc = L // cl

    def to_chunks(a):
        a = a.astype(f32).reshape((B, nc, cl) + a.shape[2:])
        return jnp.moveaxis(a, (1, 3), (0, 2))

    qc, kc, vc = to_chunks(q), to_chunks(k.astype(f32) * (DH ** -0.5)), to_chunks(v)
    ic, fc = to_chunks(ig), to_chunks(lf)
    causal = jnp.tril(jnp.ones((cl, cl), bool))

    def step(carry, xs):
        C, n, m = carry
        qb, kb, vb, ib, fb = xs
        b = jnp.cumsum(fb, axis=-1)
        log_d = jnp.where(causal, b[..., :, None] - b[..., None, :] + ib[..., None, :], -jnp.inf)
        m_inter = b + m[..., None]
        m_t = jnp.maximum(m_inter, jnp.max(log_d, axis=-1))
        d = jnp.exp(log_d - m_t[..., None])
        w_inter = jnp.exp(m_inter - m_t)
        s = jnp.einsum('bhtd,bhsd->bhts', qb, kb) * d
        num = jnp.einsum('bhts,bhsv->bhtv', s, vb) + w_inter[..., None] * jnp.einsum('bhvd,bhtd->bhtv', C, qb)
        den = jnp.sum(s, -1) + w_inter * jnp.einsum('bhd,bhtd->bht', n, qb)
        h = num / jnp.maximum(jnp.abs(den), jnp.exp(-m_t))[..., None]
        m_new = m_t[..., -1]
        w_old = jnp.exp(b[..., -1] + m - m_new)
        w_s = jnp.exp(b[..., -1:] - b + ib - m_new[..., None])
        C_new = w_old[..., None, None] * C + jnp.einsum('bhs,bhsv,bhsd->bhvd', w_s, vb, kb)
        n_new = w_old[..., None] * n + jnp.einsum('bhs,bhsd->bhd', w_s, kb)
        return (C_new, n_new, m_new), h

    (C, n, m), h = lax.scan(step, (C0.astype(f32), n0.astype(f32), m0.astype(f32)), (qc, kc, vc, ic, fc))
    h = jnp.moveaxis(h, (0, 2), (1, 3)).reshape(B, L, NH, DH)
    return h, C.astype(C0.dtype), n.astype(n0.dtype), m.astype(m0.dtype)


def grouped_experts(xf, expert_idx, expert_w, e_gate, e_up, e_down):
    T, D = xf.shape
    A = T * TOP_K
    flat_e = expert_idx.reshape(A)
    flat_t = jnp.repeat(jnp.arange(T, dtype=jnp.int32), TOP_K)
    flat_w = expert_w.reshape(A)
    order = jnp.argsort(flat_e)
    se = flat_e[order]
    counts = jnp.bincount(flat_e, length=N_EXPERTS)
    padded = (counts + MOE_BLOCK - 1) // MOE_BLOCK * MOE_BLOCK
    pad_end = jnp.cumsum(padded)
    pad_start = pad_end - padded
    start = jnp.cumsum(counts) - counts
    dest = pad_start[se] + jnp.arange(A) - start[se]
    n_blocks = -(-A // MOE_BLOCK) + N_EXPERTS
    n_rows = n_blocks * MOE_BLOCK
    row_tok = jnp.full((n_rows,), T, jnp.int32).at[dest].set(flat_t[order])
    row_w = jnp.zeros((n_rows,), xf.dtype).at[dest].set(flat_w[order].astype(xf.dtype))
    blk_expert = jnp.minimum(jnp.searchsorted(pad_end, jnp.arange(n_blocks) * MOE_BLOCK, side='right'), N_EXPERTS - 1)
    x_rows = jnp.concatenate([xf, jnp.zeros((1, D), xf.dtype)], 0)[row_tok].reshape(n_blocks, MOE_BLOCK, D)

    def run(args):
        xb, e = args
        hb = jax.nn.silu(xb @ e_gate[e]) * (xb @ e_up[e])
        return hb @ e_down[e]

    y_rows = lax.map(run, (x_rows, blk_expert)).reshape(n_rows, D)
    return jax.ops.segment_sum(y_rows * row_w[:, None], row_tok, num_segments=T + 1)[:T]


def hier_moe(xf, w_group, b_group, w_expert, b_expert, e_gate, e_up, e_down):
    T = xf.shape[0]
    g_logits = (xf @ w_group).astype(jnp.float32) + b_group
    g_prob = jax.nn.softmax(g_logits, axis=-1)
    g_sel = jnp.argmax(g_logits, axis=-1)
    g_w = jnp.max(g_prob, axis=-1, keepdims=True)
    e_logits = ((xf @ w_expert).astype(jnp.float32) + b_expert).reshape(T, N_GROUPS, EXPERTS_PER_GROUP)
    e_in = e_logits[jnp.arange(T), g_sel]
    top_w, top_i = lax.top_k(jax.nn.softmax(e_in, axis=-1), TOP_K)
    top_w = top_w / jnp.sum(top_w, -1, keepdims=True) * g_w
    expert_idx = g_sel[:, None] * EXPERTS_PER_GROUP + top_i
    return grouped_experts(xf, expert_idx, top_w, e_gate, e_up, e_down)


def hybrid_layer(x, p, pos, conv_buf, C0, n0, m0, past_ckv, past_krope, lw):
    (w_in, conv_w, q_norm_g, w_uq, kv_norm_g, w_ukv, mlstm_gate_b, mix_norm_g, w_out,
     ln1_g, ln1_b, w_group, b_group, w_expert, b_expert, e_gate, e_up, e_down,
     w_pe, w_pg, ln2_g, ln2_b) = lw
    B, L, D = x.shape
    f32 = jnp.float32
    z = jnp.einsum('bld,dn->bln', x, w_in)
    sizes = [CONV_WIDTH] * 3 + [Q_RANK, KV_RANK, QK_ROPE] + [MLSTM_WIDTH] * 4 + [MLSTM_HEADS] * 2
    cuts = [int(c) for c in np.cumsum(sizes)[:-1]]
    cb, cc, ch, c_q, c_kv, k_r, mq, mk, mv, mo, mi, mf = jnp.split(z, cuts, axis=-1)
    conv_out, conv_state = short_conv(cc * ch, conv_buf, conv_w)
    y_conv = cb * conv_out
    q = jnp.einsum('blr,rn->bln', rms_norm(c_q, q_norm_g), w_uq).reshape(B, L, MLA_HEADS, QK_NOPE + QK_ROPE)
    q_nope, q_rope = q[..., :QK_NOPE], rope(q[..., QK_NOPE:], pos)
    ckv = rms_norm(c_kv, kv_norm_g)
    krope = rope(k_r, pos)
    w_kv = w_ukv.reshape(KV_RANK, MLA_HEADS, QK_NOPE + V_DIM)
    w_uk, w_uv = w_kv[..., :QK_NOPE], w_kv[..., QK_NOPE:]
    if past_ckv is None:
        k_nope = jnp.einsum('blr,rhd->blhd', ckv, w_uk)
        v = jnp.einsum('blr,rhd->blhd', ckv, w_uv)
        att = mla_prompt_attention(q_nope, q_rope, k_nope, krope, v)
    else:
        att = mla_sample_attention(q_nope, q_rope, jnp.concatenate([past_ckv, ckv], 1),
                                   jnp.concatenate([past_krope, krope], 1), w_uk, w_uv, past_ckv.shape[1])
    y_mla = att.reshape(B, L, MLA_HEADS * V_DIM)
    ig = mi.astype(f32) + mlstm_gate_b[:MLSTM_HEADS].astype(f32)
    lf = jax.nn.log_sigmoid(mf.astype(f32) + mlstm_gate_b[MLSTM_HEADS:].astype(f32))
    heads = lambda a: a.reshape(B, L, MLSTM_HEADS, MLSTM_DH)
    h_m, C1, n1, m1 = mlstm_chunkwise(heads(mq), heads(mk), heads(mv), ig, lf, C0, n0, m0)
    y_mlstm = jax.nn.sigmoid(mo) * h_m.reshape(B, L, MLSTM_WIDTH).astype(x.dtype)
    y = jnp.concatenate([y_conv, y_mla, y_mlstm], -1).reshape(B, L, N_MIX_HEADS, HEAD_DIM)
    y = rms_norm(y, mix_norm_g.reshape(N_MIX_HEADS, HEAD_DIM)).reshape(B, L, MIX_WIDTH)
    h1 = layer_norm(DN_ALPHA * x + jnp.einsum('blm,md->bld', y, w_out), ln1_g, ln1_b)
    moe = hier_moe(h1.reshape(B * L, D), w_group, b_group, w_expert, b_expert, e_gate, e_up, e_down).reshape(B, L, D)
    ple = jnp.einsum('blp,pd->bld', p, w_pe) * jax.nn.sigmoid(jnp.einsum('bld,de->ble', h1, w_pg))
    out = layer_norm(DN_ALPHA * h1 + moe + ple, ln2_g, ln2_b)
    return out, ckv, krope, conv_state, C1, n1, m1


def setup_inputs(seed: int = 0) -> dict:
    key = jax.random.key(seed)
    ks = iter(jax.random.split(key, 48))
    nrm = lambda shape, scale=1.0: scale * jax.random.normal(next(ks), shape, jnp.float32)
    gain = lambda shape: 1.0 + nrm(shape, 0.02)
    n_pages = PAST_LEN // PAGE_SIZE
    n_phys = (DEC_BATCH * n_pages * 5 + 3) // 4
    inp = {}
    inp['x_prompt'] = nrm((BATCH, SEQ, D_MODEL))
    inp['x_sample'] = nrm((DEC_BATCH, DEC_SEQ, D_MODEL))
    inp['cache_ckv'] = nrm((DEPTH, n_phys, PAGE_SIZE, KV_RANK))
    inp['cache_krope'] = nrm((DEPTH, n_phys, PAGE_SIZE, QK_ROPE))
    inp['state_conv'] = nrm((DEPTH, DEC_BATCH, CONV_K - 1, CONV_WIDTH))
    inp['state_mlstm_C'] = nrm((DEPTH, DEC_BATCH, MLSTM_HEADS, MLSTM_DH, MLSTM_DH), 0.1)
    inp['state_mlstm_n'] = nrm((DEPTH, DEC_BATCH, MLSTM_HEADS, MLSTM_DH), 0.1)
    inp['state_mlstm_m'] = nrm((DEPTH, DEC_BATCH, MLSTM_HEADS))
    perm = jax.random.permutation(next(ks), n_phys)
    inp['page_table'] = perm[:DEC_BATCH * n_pages].reshape(DEC_BATCH, n_pages).astype(jnp.int32)
    inp['p_prompt'] = nrm((DEPTH, BATCH, SEQ, D_PLE))
    inp['p_sample'] = nrm((DEPTH, DEC_BATCH, DEC_SEQ, D_PLE))
    inp['ln0_g'] = gain((D_MODEL,))
    inp['ln0_b'] = nrm((D_MODEL,), 0.02)
    inp['w_in'] = nrm((DEPTH, D_MODEL, N_IN), D_MODEL ** -0.5)
    inp['conv_w'] = nrm((DEPTH, CONV_K, CONV_WIDTH), CONV_K ** -0.5)
    inp['q_norm_g'] = gain((DEPTH, Q_RANK))
    inp['w_uq'] = nrm((DEPTH, Q_RANK, MLA_HEADS * (QK_NOPE + QK_ROPE)), Q_RANK ** -0.5)
    inp['kv_norm_g'] = gain((DEPTH, KV_RANK))
    inp['w_ukv'] = nrm((DEPTH, KV_RANK, MLA_HEADS * (QK_NOPE + V_DIM)), KV_RANK ** -0.5)
    inp['mlstm_gate_b'] = jnp.concatenate([nrm((DEPTH, MLSTM_HEADS), 0.1),
                                           jnp.linspace(3.0, 6.0, MLSTM_HEADS)[None, :] + nrm((DEPTH, MLSTM_HEADS), 0.1)], -1)
    inp['mix_norm_g'] = gain((DEPTH, MIX_WIDTH))
    inp['w_out'] = nrm((DEPTH, MIX_WIDTH, D_MODEL), MIX_WIDTH ** -0.5 * DN_BETA)
    inp['ln1_g'] = gain((DEPTH, D_MODEL))
    inp['ln1_b'] = nrm((DEPTH, D_MODEL), 0.02)
    inp['w_group'] = nrm((DEPTH, D_MODEL, N_GROUPS), D_MODEL ** -0.5)
    inp['b_group'] = nrm((DEPTH, N_GROUPS), 0.01)
    inp['w_expert'] = nrm((DEPTH, D_MODEL, N_EXPERTS), D_MODEL ** -0.5)
    inp['b_expert'] = nrm((DEPTH, N_EXPERTS), 0.01)
    inp['e_gate'] = nrm((DEPTH, N_EXPERTS, D_MODEL, D_EXPERT), D_MODEL ** -0.5)
    inp['e_up'] = nrm((DEPTH, N_EXPERTS, D_MODEL, D_EXPERT), D_MODEL ** -0.5)
    inp['e_down'] = nrm((DEPTH, N_EXPERTS, D_EXPERT, D_MODEL), D_EXPERT ** -0.5 * DN_BETA)
    inp['w_pe'] = nrm((DEPTH, D_PLE, D_MODEL), D_PLE ** -0.5 * DN_BETA)
    inp['w_pg'] = nrm((DEPTH, D_MODEL, D_MODEL), D_MODEL ** -0.5)
    inp['ln2_g'] = gain((DEPTH, D_MODEL))
    inp['ln2_b'] = nrm((DEPTH, D_MODEL), 0.02)
    return inp


def reference(x_prompt, x_sample, cache_ckv, cache_krope, state_conv, state_mlstm_C, state_mlstm_n, state_mlstm_m,
              page_table, p_prompt, p_sample, ln0_g, ln0_b, w_in, conv_w, q_norm_g, w_uq, kv_norm_g, w_ukv,
              mlstm_gate_b, mix_norm_g, w_out, ln1_g, ln1_b, w_group, b_group, w_expert, b_expert,
              e_gate, e_up, e_down, w_pe, w_pg, ln2_g, ln2_b):
    B, S, _ = x_prompt.shape
    DB, DS, _ = x_sample.shape
    dt = x_prompt.dtype
    n_pages = PAST_LEN // PAGE_SIZE
    pos_p = jnp.arange(S)
    pos_s = PAST_LEN + jnp.arange(DS)
    hp = layer_norm(x_prompt, ln0_g, ln0_b)
    hs = layer_norm(x_sample, ln0_g, ln0_b)
    st_p, st_s = [], []
    for i in range(DEPTH):
        lw = (w_in[i], conv_w[i], q_norm_g[i], w_uq[i], kv_norm_g[i], w_ukv[i], mlstm_gate_b[i], mix_norm_g[i],
              w_out[i], ln1_g[i], ln1_b[i], w_group[i], b_group[i], w_expert[i], b_expert[i], e_gate[i], e_up[i],
              e_down[i], w_pe[i], w_pg[i], ln2_g[i], ln2_b[i])
        hp, *sp = hybrid_layer(hp, p_prompt[i], pos_p,
                               jnp.zeros((B, CONV_K - 1, CONV_WIDTH), dt),
                               jnp.zeros((B, MLSTM_HEADS, MLSTM_DH, MLSTM_DH), dt),
                               jnp.zeros((B, MLSTM_HEADS, MLSTM_DH), dt),
                               jnp.zeros((B, MLSTM_HEADS), dt), None, None, lw)
        past_ckv = cache_ckv[i][page_table].reshape(DB, n_pages * PAGE_SIZE, KV_RANK)
        past_kr = cache_krope[i][page_table].reshape(DB, n_pages * PAGE_SIZE, QK_ROPE)
        hs, *ss = hybrid_layer(hs, p_sample[i], pos_s, state_conv[i], state_mlstm_C[i], state_mlstm_n[i],
                               state_mlstm_m[i], past_ckv, past_kr, lw)
        st_p.append(sp)
        st_s.append(ss)
    ckv_p, krope_p, conv_p, C_p, n_p, m_p = [jnp.stack(a) for a in zip(*st_p)]
    ckv_s, krope_s, conv_s, C_s, n_s, m_s = [jnp.stack(a) for a in zip(*st_s)]
    return (hp, hs, ckv_p, krope_p, conv_p, C_p, n_p, m_p, ckv_s, krope_s, conv_s, C_s, n_s, m_s)
```

```python
import functools

import jax
import jax.numpy as jnp
from jax import lax
from jax.experimental import pallas as pl
from jax.experimental.pallas import tpu as pltpu

F32 = jnp.float32
BF16 = jnp.bfloat16

LANES = 128
HEAD_DIM = 64
CONV_WIDTH = 256
CONV_K = 3
MLA_HEADS = 8
Q_RANK = 256
KV_RANK = 128
QK_NOPE = 64
QK_ROPE = 32
V_DIM = 64
ROPE_HALF = QK_ROPE // 2
ROPE_THETA = 10000.0
MLSTM_HEADS = 4
MLSTM_DH = 64
MLSTM_WIDTH = MLSTM_HEADS * MLSTM_DH
N_GROUPS = 4
EXPERTS_PER_GROUP = 8
N_EXPERTS = N_GROUPS * EXPERTS_PER_GROUP
TOP_K = 2
D_EXPERT = 256
LN_EPS = 1e-5
RMS_EPS = 1e-6
QK_SCALE = (QK_NOPE + QK_ROPE) ** -0.5

Z_WIDTH = 3 * CONV_WIDTH + Q_RANK + KV_RANK + 4 * MLSTM_WIDTH + LANES
GATE_LANE = QK_ROPE
ROUTER_LANE = N_GROUPS

VMEM_LIMIT = 56 * 1024 * 1024

TOKEN_TILE = 512
ATTN_TILE = 512
MLSTM_CHUNK = 256
COMBINE_TILE = 256
PROMPT_EXPERT_BLOCK = 256
SAMPLE_EXPERT_BLOCK = 16
PAGES_PER_CHUNK = 32


def _cparams(n_grid):
    return pltpu.CompilerParams(dimension_semantics=("arbitrary",) * n_grid, vmem_limit_bytes=VMEM_LIMIT)


def _layer_norm(x, g, b):
    mu = jnp.mean(x, -1, keepdims=True)
    xc = x - mu
    var = jnp.mean(xc * xc, -1, keepdims=True)
    return xc * lax.rsqrt(var + LN_EPS) * g + b


def _rms_norm(x, g):
    return x * lax.rsqrt(jnp.mean(x * x, -1, keepdims=True) + RMS_EPS) * g


def _log_sigmoid(x):
    return jnp.minimum(x, 0.0) - jnp.log1p(jnp.exp(-jnp.abs(x)))


def _sigmoid(x):
    return 1.0 / (1.0 + jnp.exp(-x))


def _rope_block(x, cos_t, sin_t, first_lane):
    lane = lax.broadcasted_iota(jnp.int32, x.shape, 1)
    in_first = (lane >= first_lane) & (lane < first_lane + ROPE_HALF)
    partner = jnp.where(in_first, pltpu.roll(x, LANES - ROPE_HALF, 1), pltpu.roll(x, ROPE_HALF, 1))
    return x * cos_t + partner * sin_t


def _in_proj_kernel(*refs, apply_ln0, seq_mode, emit_kv, tm):
    it = iter(refs)
    x_ref = next(it)
    if apply_ln0:
        g0_ref, b0_ref = next(it), next(it)
    w_in_ref, convw_ref = next(it), next(it)
    if not seq_mode:
        hist0_ref, hist1_ref = next(it), next(it)
    qg_ref, wuq_ref, kvg_ref = next(it), next(it), next(it)
    if emit_kv:
        wuk_ref, wuv_ref = next(it), next(it)
    gb_ref, cq_ref, sq_ref, ck_ref, sk_ref = next(it), next(it), next(it), next(it), next(it)
    if apply_ln0:
        xn_ref = next(it)
    yconv_ref, q_ref, ckv_ref, krope_ref = next(it), next(it), next(it), next(it)
    mq_ref, mkT_ref, mv_ref, og_ref, gates_ref, gatesT_ref = (next(it) for _ in range(6))
    if emit_kv:
        k_ref, v_ref = next(it), next(it)
    if seq_mode:
        cstate_ref = next(it)
        ubuf = next(it)
    else:
        u_ref = next(it)

    x = x_ref[...]
    if apply_ln0:
        x = _layer_norm(x, g0_ref[...], b0_ref[...])
        xn_ref[...] = x
    z = jnp.dot(x.astype(BF16), w_in_ref[...], preferred_element_type=F32)
    o = 0
    cb = z[:, o:o + CONV_WIDTH]; o += CONV_WIDTH
    cc = z[:, o:o + CONV_WIDTH]; o += CONV_WIDTH
    ch = z[:, o:o + CONV_WIDTH]; o += CONV_WIDTH
    c_q = z[:, o:o + Q_RANK]; o += Q_RANK
    c_kv = z[:, o:o + KV_RANK]; o += KV_RANK
    mq = z[:, o:o + MLSTM_WIDTH]; o += MLSTM_WIDTH
    mk = z[:, o:o + MLSTM_WIDTH]; o += MLSTM_WIDTH
    mv = z[:, o:o + MLSTM_WIDTH]; o += MLSTM_WIDTH
    mo = z[:, o:o + MLSTM_WIDTH]; o += MLSTM_WIDTH
    last = z[:, o:o + LANES]

    u = cc * ch
    w = convw_ref[...]
    if seq_mode:
        s_idx = pl.program_id(1)

        @pl.when(s_idx == 0)
        def _():
            ubuf[0:8, :] = jnp.zeros((8, CONV_WIDTH), F32)

        ubuf[8:8 + tm, :] = u
        conv = ubuf[6:6 + tm, :] * w[0:1] + ubuf[7:7 + tm, :] * w[1:2] + u * w[2:3]
        ubuf[0:8, :] = ubuf[tm:tm + 8, :]

        @pl.when(s_idx == pl.num_programs(1) - 1)
        def _():
            cstate_ref[...] = ubuf[6:8, :]
    else:
        conv = hist0_ref[...] * w[0:1] + hist1_ref[...] * w[1:2] + u * w[2:3]
        u_ref[...] = u
    yconv_ref[...] = cb * conv

    cqn = _rms_norm(c_q, qg_ref[...])
    q = jnp.dot(cqn.astype(BF16), wuq_ref[...], preferred_element_type=F32)
    cq, sq = cq_ref[...], sq_ref[...]
    for h in range(MLA_HEADS):
        qh = _rope_block(q[:, h * LANES:(h + 1) * LANES], cq, sq, QK_NOPE) * QK_SCALE
        q_ref[:, h * LANES:(h + 1) * LANES] = qh.astype(q_ref.dtype)

    ckv = _rms_norm(c_kv, kvg_ref[...])
    ckv_ref[...] = ckv
    kr = _rope_block(last, ck_ref[...], sk_ref[...], 0)
    krope_ref[...] = kr[:, 0:QK_ROPE]
    if emit_kv:
        ckv_b = ckv.astype(BF16)
        kfull = jnp.dot(ckv_b, wuk_ref[...], preferred_element_type=F32)
        kr_shift = pltpu.roll(kr, QK_NOPE, 1)
        for h in range(MLA_HEADS):
            k_ref[:, h * LANES:(h + 1) * LANES] = (kfull[:, h * LANES:(h + 1) * LANES] + kr_shift).astype(BF16)
        v_ref[...] = jnp.dot(ckv_b, wuv_ref[...], preferred_element_type=F32).astype(BF16)

    mq_ref[...] = mq.astype(mq_ref.dtype)
    mkT_ref[...] = (mk * (MLSTM_DH ** -0.5)).T.astype(mkT_ref.dtype)
    mv_ref[...] = mv.astype(mv_ref.dtype)
    og_ref[...] = mo
    g = last + gb_ref[...]
    lane = lax.broadcasted_iota(jnp.int32, g.shape, 1)
    is_forget = (lane >= GATE_LANE + MLSTM_HEADS) & (lane < GATE_LANE + 2 * MLSTM_HEADS)
    g = jnp.where(is_forget, _log_sigmoid(g), g)
    gates_ref[...] = g[:, GATE_LANE:GATE_LANE + 2 * MLSTM_HEADS]
    gatesT_ref[...] = g.T[GATE_LANE:GATE_LANE + 2 * MLSTM_HEADS, :]


def _in_proj(x, lw, tabs, *, apply_ln0, ln0, seq_mode, emit_kv, hist=None):
    B, S, D = x.shape
    tm = min(TOKEN_TILE, S)
    assert S % tm == 0
    ns = S // tm
    tok = lambda w: pl.BlockSpec((None, tm, w), lambda b, s: (b, s, 0))
    full2 = lambda a: pl.BlockSpec(a.shape, lambda b, s: (0, 0))
    tab = pl.BlockSpec((tm, LANES), lambda b, s: (s, 0))

    ins, specs = [x], [tok(D)]
    if apply_ln0:
        ins += [ln0[0], ln0[1]]; specs += [full2(ln0[0]), full2(ln0[1])]
    ins += [lw["w_in"], lw["conv_w"]]; specs += [full2(lw["w_in"]), full2(lw["conv_w"])]
    if not seq_mode:
        ins += [hist[0], hist[1]]; specs += [tok(CONV_WIDTH), tok(CONV_WIDTH)]
    ins += [lw["q_norm_g"], lw["w_uq"], lw["kv_norm_g"]]
    specs += [full2(lw["q_norm_g"]), full2(lw["w_uq"]), full2(lw["kv_norm_g"])]
    if emit_kv:
        ins += [lw["w_uk"], lw["w_uv"]]; specs += [full2(lw["w_uk"]), full2(lw["w_uv"])]
    ins += [lw["gate_b"]] + list(tabs)
    specs += [full2(lw["gate_b"])] + [tab] * 4

    outs, ospecs = [], []

    def add(shape, dtype, spec):
        outs.append(jax.ShapeDtypeStruct(shape, dtype)); ospecs.append(spec)

    if apply_ln0:
        add((B, S, D), F32, tok(D))
    add((B, S, CONV_WIDTH), F32, tok(CONV_WIDTH))
    add((B, S, MLA_HEADS * LANES), BF16 if emit_kv else F32, tok(MLA_HEADS * LANES))
    add((B, S, KV_RANK), F32, tok(KV_RANK))
    add((B, S, QK_ROPE), F32, tok(QK_ROPE))
    mdt = BF16 if seq_mode else F32
    add((B, S, MLSTM_WIDTH), mdt, tok(MLSTM_WIDTH))
    add((B, MLSTM_WIDTH, S), mdt, pl.BlockSpec((None, MLSTM_WIDTH, tm), lambda b, s: (b, 0, s)))
    add((B, S, MLSTM_WIDTH), mdt, tok(MLSTM_WIDTH))
    add((B, S, MLSTM_WIDTH), F32, tok(MLSTM_WIDTH))
    add((B, S, 2 * MLSTM_HEADS), F32, tok(2 * MLSTM_HEADS))
    add((B, 2 * MLSTM_HEADS, S), F32, pl.BlockSpec((None, 2 * MLSTM_HEADS, tm), lambda b, s: (b, 0, s)))
    if emit_kv:
        add((B, S, MLA_HEADS * LANES), BF16, tok(MLA_HEADS * LANES))
        add((B, S, MLA_HEADS * V_DIM), BF16, tok(MLA_HEADS * V_DIM))
    scratch = []
    if seq_mode:
        add((B, CONV_K - 1, CONV_WIDTH), F32, pl.BlockSpec((None, CONV_K - 1, CONV_WIDTH), lambda b, s: (b, 0, 0)))
        scratch = [pltpu.VMEM((tm + 8, CONV_WIDTH), F32)]
    else:
        add((B, S, CONV_WIDTH), F32, tok(CONV_WIDTH))
    kern = functools.partial(_in_proj_kernel, apply_ln0=apply_ln0, seq_mode=seq_mode, emit_kv=emit_kv, tm=tm)
    return pl.pallas_call(kern, grid=(B, ns), in_specs=specs, out_specs=ospecs, out_shape=outs,
                          scratch_shapes=scratch, compiler_params=_cparams(2), name="in_proj")(*ins)


def _flash_attn_kernel(q_ref, k_ref, v_ref, o_ref, *, tq):
    qi = pl.program_id(2)
    row = lax.broadcasted_iota(jnp.int32, (tq, tq), 0)
    col = lax.broadcasted_iota(jnp.int32, (tq, tq), 1)
    causal = col <= row
    outs = []
    for hh in range(2):
        q = q_ref[:, hh * LANES:(hh + 1) * LANES]

        def body(ki, carry, hh=hh, q=q):
            m, l, acc = carry
            start = pl.multiple_of(ki * tq, tq)
            k = k_ref[pl.ds(start, tq), hh * LANES:(hh + 1) * LANES]
            v = v_ref[pl.ds(start, tq), :]
            s = lax.dot_general(q, k, (((1,), (1,)), ((), ())), preferred_element_type=F32)
            s = jnp.where(causal | (ki < qi), s, -jnp.inf)
            m_new = jnp.maximum(m, jnp.max(s, -1, keepdims=True))
            alpha = jnp.exp(m - m_new)
            p = jnp.exp(s - m_new)
            l = alpha * l + jnp.sum(p, -1, keepdims=True)
            acc = alpha * acc + jnp.dot(p.astype(BF16), v, preferred_element_type=F32)
            return m_new, l, acc

        init = (jnp.full((tq, 1), -jnp.inf, F32), jnp.zeros((tq, 1), F32), jnp.zeros((tq, LANES), F32))
        m, l, acc = lax.fori_loop(0, qi + 1, body, init)
        outs.append(acc / l)
    lane = lax.broadcasted_iota(jnp.int32, (tq, LANES), 1)
    o_ref[...] = jnp.where(lane < V_DIM, outs[0], outs[1])


def _flash_attn(q, k, v):
    B, S, _ = q.shape
    tq = min(ATTN_TILE, S)
    assert S % tq == 0
    kern = functools.partial(_flash_attn_kernel, tq=tq)
    return pl.pallas_call(
        kern, grid=(B, MLA_HEADS // 2, S // tq),
        in_specs=[pl.BlockSpec((None, tq, 2 * LANES), lambda b, h, i: (b, i, h)),
                  pl.BlockSpec((None, S, 2 * LANES), lambda b, h, i: (b, 0, h)),
                  pl.BlockSpec((None, S, 2 * V_DIM), lambda b, h, i: (b, 0, h))],
        out_specs=pl.BlockSpec((None, tq, 2 * V_DIM), lambda b, h, i: (b, i, h)),
        out_shape=jax.ShapeDtypeStruct((B, S, MLA_HEADS * V_DIM), F32),
        compiler_params=_cparams(3), name="flash_attn")(q, k, v)


def _paged_attn_kernel(pt_ref, q_ref, ckvn_ref, krn_ref, wukT_ref, wuv_ref, cckv_ref, ckr_ref, o_ref,
                       kbuf, rbuf, sems, qlat_scr, qr_scr, m_scr, l_scr, acc_scr, *, layer, npc, nchunks, page):
    b, c = pl.program_id(0), pl.program_id(1)
    g = b * nchunks + c
    total = pl.num_programs(0) * nchunks
    slot = g % 2

    def page_copies(bb, cc, sl, j):
        pg = pt_ref[bb, cc * npc + j]
        return (pltpu.make_async_copy(cckv_ref.at[layer, pg], kbuf.at[sl, j], sems.at[0, sl]),
                pltpu.make_async_copy(ckr_ref.at[layer, pg], rbuf.at[sl, j], sems.at[1, sl]))

    def start_chunk(bb, cc, sl):
        def issue(j, _):
            for cp in page_copies(bb, cc, sl, j):
                cp.start()
            return 0
        lax.fori_loop(0, npc, issue, 0)

    @pl.when(g == 0)
    def _():
        start_chunk(b, c, slot)

    @pl.when(g + 1 < total)
    def _():
        nxt = g + 1
        start_chunk(nxt // nchunks, nxt % nchunks, 1 - slot)

    @pl.when(c == 0)
    def _():
        qrow = q_ref[0]
        for h in range(MLA_HEADS):
            qn = jnp.broadcast_to(qrow[:, h * LANES:h * LANES + QK_NOPE], (8, QK_NOPE)).astype(BF16)
            qlat_scr[h:h + 1, :] = jnp.dot(qn, wukT_ref[h], preferred_element_type=F32)[0:1]
            qr_scr[h:h + 1, :] = qrow[:, h * LANES + QK_NOPE:h * LANES + QK_NOPE + QK_ROPE]
        m_scr[...] = jnp.full(m_scr.shape, -jnp.inf, F32)
        l_scr[...] = jnp.zeros(l_scr.shape, F32)
        acc_scr[...] = jnp.zeros(acc_scr.shape, F32)

    def wait(j, _):
        for cp in page_copies(b, c, slot, j):
            cp.wait()
        return 0
    lax.fori_loop(0, npc, wait, 0)

    kb = kbuf[slot].reshape(npc * page, KV_RANK).astype(BF16)
    rb = rbuf[slot].reshape(npc * page, QK_ROPE).astype(BF16)
    nt = (((1,), (1,)), ((), ()))
    s = (lax.dot_general(qlat_scr[...].astype(BF16), kb, nt, preferred_element_type=F32)
         + lax.dot_general(qr_scr[...].astype(BF16), rb, nt, preferred_element_type=F32))
    m = m_scr[...]
    m_new = jnp.maximum(m, jnp.max(s, -1, keepdims=True))
    alpha = jnp.exp(m - m_new)
    p = jnp.exp(s - m_new)
    l_scr[...] = alpha * l_scr[...] + jnp.sum(p, -1, keepdims=True)
    acc_scr[...] = alpha * acc_scr[...] + jnp.dot(p.astype(BF16), kb, preferred_element_type=F32)
    m_scr[...] = m_new

    @pl.when(c == nchunks - 1)
    def _():
        ckv_new, kr_new = ckvn_ref[0], krn_ref[0]
        s_new = (jnp.sum(qlat_scr[...] * ckv_new, -1, keepdims=True)
                 + jnp.sum(qr_scr[...] * kr_new, -1, keepdims=True))
        m_old = m_scr[...]
        m_fin = jnp.maximum(m_old, s_new)
        a = jnp.exp(m_old - m_fin)
        p_new = jnp.exp(s_new - m_fin)
        l_fin = a * l_scr[...] + p_new
        o_lat = (a * acc_scr[...] + p_new * ckv_new) / l_fin
        for h in range(MLA_HEADS):
            oh = jnp.broadcast_to(o_lat[h:h + 1, :], (8, KV_RANK)).astype(BF16)
            o_ref[0, :, h * V_DIM:(h + 1) * V_DIM] = jnp.dot(oh, wuv_ref[h], preferred_element_type=F32)[0:1]


def _paged_attn(q, ckv_new, kr_new, wukT, wuv_h, cache_ckv, cache_krope, page_table, layer):
    DB = q.shape[0]
    n_pages = page_table.shape[1]
    page = cache_ckv.shape[2]
    npc = min(PAGES_PER_CHUNK, n_pages)
    assert n_pages % npc == 0
    nchunks = n_pages // npc
    kern = functools.partial(_paged_attn_kernel, layer=layer, npc=npc, nchunks=nchunks, page=page)
    row = lambda w: pl.BlockSpec((1, 1, w), lambda b, c, pt: (b, 0, 0))
    full3 = lambda a: pl.BlockSpec(a.shape, lambda b, c, pt: (0, 0, 0))
    gs = pltpu.PrefetchScalarGridSpec(
        num_scalar_prefetch=1, grid=(DB, nchunks),
        in_specs=[row(MLA_HEADS * LANES), row(KV_RANK), row(QK_ROPE), full3(wukT), full3(wuv_h),
                  pl.BlockSpec(memory_space=pl.ANY), pl.BlockSpec(memory_space=pl.ANY)],
        out_specs=row(MLA_HEADS * V_DIM),
        scratch_shapes=[pltpu.VMEM((2, npc, page, KV_RANK), F32), pltpu.VMEM((2, npc, page, QK_ROPE), F32),
                        pltpu.SemaphoreType.DMA((2, 2)),
                        pltpu.VMEM((MLA_HEADS, KV_RANK), F32), pltpu.VMEM((MLA_HEADS, QK_ROPE), F32),
                        pltpu.VMEM((MLA_HEADS, 1), F32), pltpu.VMEM((MLA_HEADS, 1), F32),
                        pltpu.VMEM((MLA_HEADS, KV_RANK), F32)])
    return pl.pallas_call(kern, grid_spec=gs, out_shape=jax.ShapeDtypeStruct((DB, 1, MLA_HEADS * V_DIM), F32),
                          compiler_params=_cparams(2), name="paged_attn")(
        page_table, q, ckv_new, kr_new, wukT, wuv_h, cache_ckv, cache_krope)


def _mlstm_seq_kernel(q_ref, kT_ref, v_ref, g_ref, gT_ref, h_ref, ct_out, n_out, m_out, ct_scr, n_scr, m_scr, *, L):
    c = pl.program_id(1)

    @pl.when(c == 0)
    def _():
        ct_scr[...] = jnp.zeros(ct_scr.shape, F32)
        n_scr[...] = jnp.zeros(n_scr.shape, F32)
        m_scr[...] = jnp.zeros(m_scr.shape, F32)

    g = g_ref[...]
    gT = gT_ref[...]
    row = lax.broadcasted_iota(jnp.int32, (L, L), 0)
    col = lax.broadcasted_iota(jnp.int32, (L, L), 1)
    causal = col <= row
    lower = causal.astype(F32)
    upper = (row <= col).astype(F32)
    b_cols = jnp.dot(lower, g, precision=lax.Precision.HIGHEST, preferred_element_type=F32)
    b_rows = jnp.dot(gT, upper, precision=lax.Precision.HIGHEST, preferred_element_type=F32)
    for h in range(MLSTM_HEADS):
        sl = slice(h * MLSTM_DH, (h + 1) * MLSTM_DH)
        qh, khT, vh = q_ref[:, sl], kT_ref[sl, :], v_ref[:, sl]
        b_col = b_cols[:, MLSTM_HEADS + h:MLSTM_HEADS + h + 1]
        b_row = b_rows[MLSTM_HEADS + h:MLSTM_HEADS + h + 1, :]
        i_row, i_col = gT[h:h + 1, :], g[:, h:h + 1]
        m_prev = m_scr[h:h + 1, 0:1]
        log_d = jnp.where(causal, b_col - b_row + i_row, -jnp.inf)
        m_inter = b_col + m_prev
        m_t = jnp.maximum(m_inter, jnp.max(log_d, -1, keepdims=True))
        d = jnp.exp(log_d - m_t)
        w_inter = jnp.exp(m_inter - m_t)
        s = jnp.dot(qh, khT, preferred_element_type=F32) * d
        ct = ct_scr[h]
        n = n_scr[h:h + 1, :]
        num = (jnp.dot(s.astype(BF16), vh, preferred_element_type=F32)
               + w_inter * jnp.dot(qh, ct.astype(BF16), preferred_element_type=F32))
        den = jnp.sum(s, -1, keepdims=True) + w_inter * jnp.sum(qh.astype(F32) * n, -1, keepdims=True)
        h_ref[:, sl] = num / jnp.maximum(jnp.abs(den), jnp.exp(-m_t))
        m_new = m_t[L - 1:L, :]
        b_last = b_col[L - 1:L, :]
        w_old = jnp.exp(b_last + m_prev - m_new)
        w_s = jnp.exp(b_last - b_col + i_col - m_new)
        ct_scr[h] = w_old * ct + jnp.dot(khT, (w_s * vh.astype(F32)).astype(BF16), preferred_element_type=F32)
        w_s_row = jnp.exp(b_last - b_row + i_row - m_new)
        k_sum = jnp.sum(khT.astype(F32) * w_s_row, -1, keepdims=True)
        n_scr[h:h + 1, :] = w_old * n + _col_to_row(k_sum)
        m_scr[h:h + 1, :] = jnp.broadcast_to(m_new, (1, LANES))

    @pl.when(c == pl.num_programs(1) - 1)
    def _():
        ct_out[...] = ct_scr[...]
        n_out[...] = n_scr[...]
        m_out[...] = m_scr[...]


def _eye(n):
    return (lax.broadcasted_iota(jnp.int32, (n, n), 0) == lax.broadcasted_iota(jnp.int32, (n, n), 1)).astype(F32)


def _col_to_row(col):
    return jnp.sum(_eye(col.shape[0]) * col, 0, keepdims=True)


def _row_to_col(row):
    return jnp.sum(_eye(row.shape[1]) * row, 1, keepdims=True)


def _mlstm_seq(mq, mkT, mv, gates, gatesT):
    B, S, _ = mq.shape
    L = min(MLSTM_CHUNK, S)
    assert S % L == 0
    kern = functools.partial(_mlstm_seq_kernel, L=L)
    tok = lambda w: pl.BlockSpec((None, L, w), lambda b, c: (b, c, 0))
    tokT = lambda w: pl.BlockSpec((None, w, L), lambda b, c: (b, 0, c))
    st = lambda *dims: pl.BlockSpec((None,) + dims, lambda b, c: (b,) + (0,) * len(dims))
    return pl.pallas_call(
        kern, grid=(B, S // L),
        in_specs=[tok(MLSTM_WIDTH), tokT(MLSTM_WIDTH), tok(MLSTM_WIDTH), tok(2 * MLSTM_HEADS), tokT(2 * MLSTM_HEADS)],
        out_specs=[tok(MLSTM_WIDTH), st(MLSTM_HEADS, MLSTM_DH, MLSTM_DH), st(MLSTM_HEADS, MLSTM_DH), st(8, LANES)],
        out_shape=[jax.ShapeDtypeStruct((B, S, MLSTM_WIDTH), F32),
                   jax.ShapeDtypeStruct((B, MLSTM_HEADS, MLSTM_DH, MLSTM_DH), F32),
                   jax.ShapeDtypeStruct((B, MLSTM_HEADS, MLSTM_DH), F32),
                   jax.ShapeDtypeStruct((B, 8, LANES), F32)],
        scratch_shapes=[pltpu.VMEM((MLSTM_HEADS, MLSTM_DH, MLSTM_DH), F32), pltpu.VMEM((MLSTM_HEADS, MLSTM_DH), F32),
                        pltpu.VMEM((8, LANES), F32)],
        compiler_params=_cparams(2), name="mlstm_seq")(mq, mkT, mv, gates, gatesT)


def _mlstm_step_kernel(q_ref, k_ref, v_ref, g_ref, c_ref, n_ref, m_ref, h_ref, c_out, n_out, m_out):
    g = g_ref[0]
    m_all = m_ref[0]
    for h in range(MLSTM_HEADS):
        sl = slice(h * MLSTM_DH, (h + 1) * MLSTM_DH)
        q, k, v = q_ref[0][:, sl], k_ref[0][:, sl], v_ref[0][:, sl]
        C, n = c_ref[0, h], n_ref[0, h:h + 1, :]
        ig, lf = g[:, h:h + 1], g[:, MLSTM_HEADS + h:MLSTM_HEADS + h + 1]
        m_prev = m_all[:, h:h + 1]
        m_inter = lf + m_prev
        m_t = jnp.maximum(m_inter, ig)
        d = jnp.exp(ig - m_t)
        w_inter = jnp.exp(m_inter - m_t)
        s = jnp.sum(q * k, -1, keepdims=True) * d
        cq = jnp.sum(C * q, -1, keepdims=True)
        v_col = _row_to_col(v)
        num = s * v_col + w_inter * cq
        den = s + w_inter * jnp.sum(n * q, -1, keepdims=True)
        h_col = num / jnp.maximum(jnp.abs(den), jnp.exp(-m_t))
        h_ref[0, :, sl] = _col_to_row(h_col)
        c_out[0, h] = w_inter * C + d * (v_col * k)
        n_out[0, h:h + 1, :] = w_inter * n + d * k
        m_out[0, :, h:h + 1] = m_t


def _mlstm_step(mq, mk, mv, gates, C0, n0, m0):
    DB = mq.shape[0]
    r3 = lambda a: a.reshape(DB, 1, a.shape[-1])
    row = lambda w: pl.BlockSpec((1, 1, w), lambda b: (b, 0, 0))
    cspec = pl.BlockSpec((1, MLSTM_HEADS, MLSTM_DH, MLSTM_DH), lambda b: (b, 0, 0, 0))
    nspec = pl.BlockSpec((1, MLSTM_HEADS, MLSTM_DH), lambda b: (b, 0, 0))
    return pl.pallas_call(
        _mlstm_step_kernel, grid=(DB,),
        in_specs=[row(MLSTM_WIDTH)] * 3 + [row(2 * MLSTM_HEADS), cspec, nspec, row(MLSTM_HEADS)],
        out_specs=[row(MLSTM_WIDTH), cspec, nspec, row(MLSTM_HEADS)],
        out_shape=[jax.ShapeDtypeStruct((DB, 1, MLSTM_WIDTH), F32), jax.ShapeDtypeStruct(C0.shape, F32),
                   jax.ShapeDtypeStruct(n0.shape, F32), jax.ShapeDtypeStruct((DB, 1, MLSTM_HEADS), F32)],
        compiler_params=_cparams(1), name="mlstm_step")(r3(mq), r3(mk), r3(mv), r3(gates), C0, n0, r3(m0))


def _mix_out_kernel(yc_ref, att_ref, hm_ref, og_ref, x_ref, mg_ref, wout_ref, g1_ref, b1_ref, wr_ref, br_ref,
                    h1_ref, route_ref, cnt_ref, ymix_scr, carry_scr, *, tm, alpha):
    @pl.when(pl.program_id(0) == 0)
    def _():
        carry_scr[...] = jnp.zeros(carry_scr.shape, F32)

    lane = lax.broadcasted_iota(jnp.int32, (tm, LANES), 1)
    low = lane < HEAD_DIM

    def norm_store(y, j):
        y2 = y * y
        s_lo = jnp.sum(jnp.where(low, y2, 0.0), -1, keepdims=True)
        s_hi = jnp.sum(jnp.where(low, 0.0, y2), -1, keepdims=True)
        inv = jnp.where(low, lax.rsqrt(s_lo / HEAD_DIM + RMS_EPS), lax.rsqrt(s_hi / HEAD_DIM + RMS_EPS))
        ymix_scr[:, j * LANES:(j + 1) * LANES] = (y * inv * mg_ref[:, j * LANES:(j + 1) * LANES]).astype(BF16)

    j = 0
    for src, width in ((yc_ref, CONV_WIDTH), (att_ref, MLA_HEADS * V_DIM)):
        for t in range(width // LANES):
            norm_store(src[:, t * LANES:(t + 1) * LANES], j)
            j += 1
    for t in range(MLSTM_WIDTH // LANES):
        sl = slice(t * LANES, (t + 1) * LANES)
        norm_store(_sigmoid(og_ref[:, sl]) * hm_ref[:, sl], j)
        j += 1

    proj = jnp.dot(ymix_scr[...], wout_ref[...], preferred_element_type=F32)
    h1 = _layer_norm(alpha * x_ref[...] + proj, g1_ref[...], b1_ref[...])
    h1_ref[...] = h1

    logits = jnp.dot(h1, wr_ref[...], precision=lax.Precision.HIGHEST, preferred_element_type=F32) + br_ref[...]
    lanef = lane.astype(F32)
    big = float(LANES)
    neg = -jnp.inf
    is_g = lane < N_GROUPS
    gl = jnp.where(is_g, logits, neg)
    gmax = jnp.max(gl, -1, keepdims=True)
    g_sel = jnp.min(jnp.where(gl == gmax, lanef, big), -1, keepdims=True)
    g_w = 1.0 / jnp.sum(jnp.where(is_g, jnp.exp(logits - gmax), 0.0), -1, keepdims=True)
    lo = ROUTER_LANE + EXPERTS_PER_GROUP * g_sel
    in_grp = (lanef >= lo) & (lanef < lo + EXPERTS_PER_GROUP)
    el = jnp.where(in_grp, logits, neg)
    e1 = jnp.max(el, -1, keepdims=True)
    i1 = jnp.min(jnp.where(el == e1, lanef, big), -1, keepdims=True)
    el2 = jnp.where(lanef == i1, neg, el)
    e2 = jnp.max(el2, -1, keepdims=True)
    i2 = jnp.min(jnp.where(el2 == e2, lanef, big), -1, keepdims=True)
    zsum = jnp.sum(jnp.where(in_grp, jnp.exp(logits - e1), 0.0), -1, keepdims=True)
    p1 = 1.0 / zsum
    p2 = jnp.exp(e2 - e1) / zsum
    psum = p1 + p2
    w1 = p1 / psum * g_w
    w2 = p2 / psum * g_w

    oh1 = (lanef == i1).astype(F32)
    oh2 = (lanef == i2).astype(F32)
    both = oh1 + oh2
    r = lax.broadcasted_iota(jnp.int32, (tm, tm), 0)
    c = lax.broadcasted_iota(jnp.int32, (tm, tm), 1)
    before = jnp.dot((c < r).astype(BF16), both.astype(BF16), preferred_element_type=F32) + carry_scr[0:1, :]
    r1 = jnp.sum(oh1 * before, -1, keepdims=True)
    r2 = jnp.sum(oh2 * before, -1, keepdims=True)
    carry = carry_scr[0:1, :] + jnp.sum(both, 0, keepdims=True)
    carry_scr[...] = jnp.broadcast_to(carry, carry_scr.shape)
    cnt_ref[...] = jnp.broadcast_to(carry, cnt_ref.shape)

    vals = (i1 - ROUTER_LANE, i2 - ROUTER_LANE, r1, r2, w1, w2)
    packed = jnp.zeros((tm, LANES), F32)
    for idx, val in enumerate(vals):
        packed = jnp.where(lane == idx, val, packed)
    route_ref[...] = packed[:, 0:8]


def _mix_out(yconv, att, hm, og, x, lw, alpha):
    T, D = x.shape
    tm = min(TOKEN_TILE, T)
    assert T % tm == 0
    tok = lambda w: pl.BlockSpec((tm, w), lambda i: (i, 0))
    full = lambda a: pl.BlockSpec(a.shape, lambda i: (0, 0))
    ws = [lw["mix_norm_g"], lw["w_out"], lw["ln1_g"], lw["ln1_b"], lw["w_router"], lw["b_router"]]
    kern = functools.partial(_mix_out_kernel, tm=tm, alpha=alpha)
    return pl.pallas_call(
        kern, grid=(T // tm,),
        in_specs=[tok(CONV_WIDTH), tok(MLA_HEADS * V_DIM), tok(MLSTM_WIDTH), tok(MLSTM_WIDTH), tok(D)] + [full(a) for a in ws],
        out_specs=[tok(D), tok(8), pl.BlockSpec((8, LANES), lambda i: (0, 0))],
        out_shape=[jax.ShapeDtypeStruct((T, D), F32), jax.ShapeDtypeStruct((T, 8), F32),
                   jax.ShapeDtypeStruct((8, LANES), F32)],
        scratch_shapes=[pltpu.VMEM((tm, D), BF16), pltpu.VMEM((8, LANES), F32)],
        compiler_params=_cparams(1), name="mix_out")(yconv, att, hm, og, x, *ws)


def _ple_dispatch_kernel(dest_ref, h1_ref, p_ref, wpe_ref, wpg_ref, h1_hbm, xs_in, ple_ref, xs_out, sem, *, tm):
    del xs_in
    base = pl.program_id(0) * tm

    def row_copy(r, k):
        return pltpu.make_async_copy(h1_hbm.at[pl.ds(base + r, 1)], xs_out.at[pl.ds(dest_ref[TOP_K * r + k], 1)], sem)

    def issue(r, _):
        for k in range(TOP_K):
            row_copy(r, k).start()
        return 0
    lax.fori_loop(0, tm, issue, 0)

    emb = jnp.dot(p_ref[...].astype(BF16), wpe_ref[...], preferred_element_type=F32)
    gate = jnp.dot(h1_ref[...].astype(BF16), wpg_ref[...], preferred_element_type=F32)
    ple_ref[...] = emb * _sigmoid(gate)

    def wait(r, _):
        for k in range(TOP_K):
            row_copy(r, k).wait()
        return 0
    lax.fori_loop(0, tm, wait, 0)


def _ple_dispatch(dest_flat, h1, p, lw, n_rows):
    T, D = h1.shape
    tm = min(TOKEN_TILE, T)
    tok = lambda w: pl.BlockSpec((tm, w), lambda i: (i, 0))
    full = lambda a: pl.BlockSpec(a.shape, lambda i: (0, 0))
    xs0 = jnp.zeros((n_rows, D), F32)
    kern = functools.partial(_ple_dispatch_kernel, tm=tm)
    ple, xs = pl.pallas_call(
        kern, grid=(T // tm,),
        in_specs=[pl.BlockSpec((TOP_K * tm,), lambda i: (i,), memory_space=pltpu.SMEM),
                  tok(D), tok(p.shape[1]), full(lw["w_pe"]), full(lw["w_pg"]),
                  pl.BlockSpec(memory_space=pl.ANY), pl.BlockSpec(memory_space=pl.ANY)],
        out_specs=[tok(D), pl.BlockSpec(memory_space=pl.ANY)],
        out_shape=[jax.ShapeDtypeStruct((T, D), F32), jax.ShapeDtypeStruct((n_rows, D), F32)],
        scratch_shapes=[pltpu.SemaphoreType.DMA(())],
        input_output_aliases={6: 1},
        compiler_params=_cparams(1), name="ple_dispatch")(dest_flat, h1, p, lw["w_pe"], lw["w_pg"], h1, xs0)
    return ple, xs


def _experts_kernel(be_ref, nu_ref, xs_ref, wgu_ref, wd_ref, ys_ref):
    del be_ref
    i = pl.program_id(0)

    @pl.when(i < nu_ref[0])
    def _():
        gu = jnp.dot(xs_ref[...].astype(BF16), wgu_ref[0], preferred_element_type=F32)
        gt, up = gu[:, 0:D_EXPERT], gu[:, D_EXPERT:2 * D_EXPERT]
        hb = gt * _sigmoid(gt) * up
        ys_ref[...] = jnp.dot(hb.astype(BF16), wd_ref[0], preferred_element_type=F32)

    @pl.when(i >= nu_ref[0])
    def _():
        ys_ref[...] = jnp.zeros(ys_ref.shape, F32)


def _experts(blk_expert, n_used, xs, lw, blk):
    n_rows, D = xs.shape
    gs = pltpu.PrefetchScalarGridSpec(
        num_scalar_prefetch=2, grid=(n_rows // blk,),
        in_specs=[pl.BlockSpec((blk, D), lambda i, be, nu: (i, 0)),
                  pl.BlockSpec((1, D, 2 * D_EXPERT), lambda i, be, nu: (be[i], 0, 0)),
                  pl.BlockSpec((1, D_EXPERT, D), lambda i, be, nu: (be[i], 0, 0))],
        out_specs=pl.BlockSpec((blk, D), lambda i, be, nu: (i, 0)))
    return pl.pallas_call(_experts_kernel, grid_spec=gs, out_shape=jax.ShapeDtypeStruct((n_rows, D), F32),
                          compiler_params=_cparams(1), name="experts")(blk_expert, n_used, xs, lw["e_gu"], lw["e_down"])


def _combine_kernel(dest_ref, h1_ref, ple_ref, route_ref, g2_ref, b2_ref, ys_hbm, out_ref, gbuf, sem, *, tm, alpha):
    def row_copy(r, k):
        return pltpu.make_async_copy(ys_hbm.at[pl.ds(dest_ref[TOP_K * r + k], 1)], gbuf.at[k, pl.ds(r, 1)], sem)

    def issue(r, _):
        for k in range(TOP_K):
            row_copy(r, k).start()
        return 0
    lax.fori_loop(0, tm, issue, 0)

    def wait(r, _):
        for k in range(TOP_K):
            row_copy(r, k).wait()
        return 0
    lax.fori_loop(0, tm, wait, 0)

    route = route_ref[...]
    moe = gbuf[0] * route[:, 4:5] + gbuf[1] * route[:, 5:6]
    out_ref[...] = _layer_norm(alpha * h1_ref[...] + moe + ple_ref[...], g2_ref[...], b2_ref[...])


def _combine(dest_flat, h1, ple, route, ys, lw, alpha):
    T, D = h1.shape
    tm = min(COMBINE_TILE, T)
    tok = lambda w: pl.BlockSpec((tm, w), lambda i: (i, 0))
    full = lambda a: pl.BlockSpec(a.shape, lambda i: (0, 0))
    kern = functools.partial(_combine_kernel, tm=tm, alpha=alpha)
    return pl.pallas_call(
        kern, grid=(T // tm,),
        in_specs=[pl.BlockSpec((TOP_K * tm,), lambda i: (i,), memory_space=pltpu.SMEM),
                  tok(D), tok(D), tok(8), full(lw["ln2_g"]), full(lw["ln2_b"]), pl.BlockSpec(memory_space=pl.ANY)],
        out_specs=tok(D), out_shape=jax.ShapeDtypeStruct((T, D), F32),
        scratch_shapes=[pltpu.VMEM((TOP_K, tm, D), F32), pltpu.SemaphoreType.DMA(())],
        compiler_params=_cparams(1), name="combine")(dest_flat, h1, ple, route, lw["ln2_g"], lw["ln2_b"], ys)


def _channel_mixer(yconv, att, hm, og, x, p, lw, alpha, blk):
    T, D = x.shape
    h1, route, cnt = _mix_out(yconv, att, hm, og, x, lw, alpha)
    counts = cnt[0, ROUTER_LANE:ROUTER_LANE + N_EXPERTS].astype(jnp.int32)
    padded = (counts + blk - 1) // blk * blk
    pad_end = jnp.cumsum(padded)
    pad_start = pad_end - padded
    n_blocks = -(-T * TOP_K // blk) + N_EXPERTS
    eid = route[:, 0:TOP_K].astype(jnp.int32)
    rank = route[:, TOP_K:2 * TOP_K].astype(jnp.int32)
    dest_flat = (pad_start[eid] + rank).reshape(T * TOP_K)
    blk_expert = jnp.minimum(jnp.searchsorted(pad_end, jnp.arange(n_blocks, dtype=jnp.int32) * blk, side="right"),
                             N_EXPERTS - 1).astype(jnp.int32)
    n_used = (pad_end[-1:] // blk).astype(jnp.int32)
    ple, xs = _ple_dispatch(dest_flat, h1, p, lw, n_blocks * blk)
    ys = _experts(blk_expert, n_used, xs, lw, blk)
    return _combine(dest_flat, h1, ple, route, ys, lw, alpha)


def _rope_tables(pos, n_rows):
    inv = ROPE_THETA ** (-jnp.arange(ROPE_HALF, dtype=F32) / ROPE_HALF)
    ang = pos.astype(F32)[:, None] * inv
    cos, sin = jnp.cos(ang), jnp.sin(ang)
    if cos.shape[0] != n_rows:
        cos, sin = jnp.broadcast_to(cos, (n_rows, ROPE_HALF)), jnp.broadcast_to(sin, (n_rows, ROPE_HALF))
    z = lambda w: jnp.zeros((n_rows, w), F32)
    one = jnp.ones((n_rows, QK_NOPE), F32)
    pad_q = LANES - QK_NOPE - QK_ROPE
    cq = jnp.concatenate([one, cos, cos, z(pad_q)], -1)
    sq = jnp.concatenate([z(QK_NOPE), -sin, sin, z(pad_q)], -1)
    ck = jnp.concatenate([cos, cos, z(LANES - QK_ROPE)], -1)
    sk = jnp.concatenate([-sin, sin, z(LANES - QK_ROPE)], -1)
    return cq, sq, ck, sk


def _prep_layer(i, w_in, conv_w, q_norm_g, w_uq, kv_norm_g, w_ukv, mlstm_gate_b, mix_norm_g, w_out, ln1_g, ln1_b,
                w_group, b_group, w_expert, b_expert, e_gate, e_up, e_down, w_pe, w_pg, ln2_g, ln2_b):
    D = w_in.shape[1]
    cuts, o = [], 0
    for wdt in [CONV_WIDTH] * 3 + [Q_RANK, KV_RANK, QK_ROPE] + [MLSTM_WIDTH] * 4 + [MLSTM_HEADS] * 2:
        cuts.append((o, o + wdt)); o += wdt
    wi = w_in[i]
    col = lambda j: wi[:, cuts[j][0]:cuts[j][1]]
    tail_pad = jnp.zeros((D, LANES - QK_ROPE - 2 * MLSTM_HEADS), F32)
    w_in_p = jnp.concatenate([col(0), col(1), col(2), col(3), col(4), col(6), col(7), col(8), col(9),
                              col(5), col(10), col(11), tail_pad], -1).astype(BF16)
    wq = w_uq[i].reshape(Q_RANK, MLA_HEADS, QK_NOPE + QK_ROPE)
    w_uq_p = jnp.pad(wq, ((0, 0), (0, 0), (0, LANES - QK_NOPE - QK_ROPE))).reshape(Q_RANK, MLA_HEADS * LANES).astype(BF16)
    wkv = w_ukv[i].reshape(KV_RANK, MLA_HEADS, QK_NOPE + V_DIM)
    w_uk, w_uv = wkv[..., :QK_NOPE], wkv[..., QK_NOPE:]
    w_uk_p = jnp.pad(w_uk, ((0, 0), (0, 0), (0, LANES - QK_NOPE))).reshape(KV_RANK, MLA_HEADS * LANES).astype(BF16)
    gate_b = jnp.concatenate([jnp.zeros((QK_ROPE,), F32), mlstm_gate_b[i],
                              jnp.zeros((LANES - QK_ROPE - 2 * MLSTM_HEADS,), F32)])[None, :]
    w_router = jnp.concatenate([w_group[i], w_expert[i], jnp.zeros((D, LANES - N_GROUPS - N_EXPERTS), F32)], -1)
    b_router = jnp.concatenate([b_group[i], b_expert[i], jnp.zeros((LANES - N_GROUPS - N_EXPERTS,), F32)])[None, :]
    return dict(
        w_in=w_in_p, conv_w=conv_w[i], q_norm_g=q_norm_g[i][None, :], w_uq=w_uq_p, kv_norm_g=kv_norm_g[i][None, :],
        w_uk=w_uk_p, w_uv=w_uv.reshape(KV_RANK, MLA_HEADS * V_DIM).astype(BF16),
        w_ukT_h=jnp.transpose(w_uk, (1, 2, 0)).astype(BF16),
        w_uv_h=jnp.transpose(w_uv, (1, 0, 2)).astype(BF16),
        gate_b=gate_b, mix_norm_g=mix_norm_g[i][None, :], w_out=w_out[i].astype(BF16),
        ln1_g=ln1_g[i][None, :], ln1_b=ln1_b[i][None, :], w_router=w_router, b_router=b_router,
        e_gu=jnp.concatenate([e_gate[i], e_up[i]], -1).astype(BF16), e_down=e_down[i].astype(BF16),
        w_pe=w_pe[i].astype(BF16), w_pg=w_pg[i].astype(BF16), ln2_g=ln2_g[i][None, :], ln2_b=ln2_b[i][None, :])


def kernel(x_prompt, x_sample, cache_ckv, cache_krope, state_conv, state_mlstm_C, state_mlstm_n, state_mlstm_m,
           page_table, p_prompt, p_sample, ln0_g, ln0_b, w_in, conv_w, q_norm_g, w_uq, kv_norm_g, w_ukv,
           mlstm_gate_b, mix_norm_g, w_out, ln1_g, ln1_b, w_group, b_group, w_expert, b_expert,
           e_gate, e_up, e_down, w_pe, w_pg, ln2_g, ln2_b):
    B, S, D = x_prompt.shape
    DB, DS, _ = x_sample.shape
    assert DS == 1
    depth = w_in.shape[0]
    alpha = (2 * depth) ** 0.25
    past_len = page_table.shape[1] * cache_ckv.shape[2]
    ln0 = (ln0_g[None, :], ln0_b[None, :])
    tabs_p = _rope_tables(jnp.arange(S), S)
    tabs_s = _rope_tables(jnp.full((1,), past_len), DB)

    hp = x_prompt
    hs = x_sample.reshape(1, DB, D)
    st_p, st_s = [], []
    for i in range(depth):
        lw = _prep_layer(i, w_in, conv_w, q_norm_g, w_uq, kv_norm_g, w_ukv, mlstm_gate_b, mix_norm_g, w_out,
                         ln1_g, ln1_b, w_group, b_group, w_expert, b_expert, e_gate, e_up, e_down, w_pe, w_pg,
                         ln2_g, ln2_b)
        first = i == 0
        outs = _in_proj(hp, lw, tabs_p, apply_ln0=first, ln0=ln0, seq_mode=True, emit_kv=True)
        if first:
            hp, outs = outs[0], outs[1:]
        yconv, q, ckv, krope, mq, mkT, mv, og, gates, gatesT, k, v, conv_state = outs
        att = _flash_attn(q, k, v)
        hm, ct, n1, m1 = _mlstm_seq(mq, mkT, mv, gates, gatesT)
        T = B * S
        flat = lambda a: a.reshape(T, a.shape[-1])
        hp = _channel_mixer(flat(yconv), flat(att), flat(hm), flat(og), flat(hp), flat(p_prompt[i]), lw, alpha,
                            PROMPT_EXPERT_BLOCK).reshape(B, S, D)
        st_p.append((ckv, krope, conv_state, jnp.swapaxes(ct, -1, -2), n1, m1[:, :MLSTM_HEADS, 0]))
        hist = (state_conv[i][None, :, 0, :], state_conv[i][None, :, 1, :])
        outs = _in_proj(hs, lw, tabs_s, apply_ln0=first, ln0=ln0, seq_mode=False, emit_kv=False, hist=hist)
        if first:
            hs, outs = outs[0], outs[1:]
        yconv, q, ckv, krope, mq, mkT, mv, og, gates, gatesT, u = outs
        r3 = lambda a: a.reshape(DB, 1, a.shape[-1])
        att = _paged_attn(r3(q), r3(ckv), r3(krope), lw["w_ukT_h"], lw["w_uv_h"], cache_ckv, cache_krope,
                          page_table, i)
        mk = jnp.swapaxes(mkT[0], 0, 1)
        hm, C1, n1, m1 = _mlstm_step(mq[0].astype(F32), mk.astype(F32), mv[0].astype(F32), gates[0],
                                     state_mlstm_C[i], state_mlstm_n[i], state_mlstm_m[i])
        hs = _channel_mixer(yconv[0], att.reshape(DB, -1), hm.reshape(DB, -1), og[0], hs[0], p_sample[i].reshape(DB, -1),
                            lw, alpha, SAMPLE_EXPERT_BLOCK).reshape(1, DB, D)
        conv_new = jnp.stack([state_conv[i][:, 1, :], u[0]], axis=1)
        st_s.append((ckv.reshape(DB, 1, -1), krope.reshape(DB, 1, -1), conv_new, C1, n1, m1.reshape(DB, MLSTM_HEADS)))
    ckv_p, krope_p, conv_p, C_p, n_p, m_p = [jnp.stack(a) for a in zip(*st_p)]
    ckv_s, krope_s, conv_s, C_s, n_s, m_s = [jnp.stack(a) for a in zip(*st_s)]
    return (hp, hs.reshape(DB, DS, D), ckv_p, krope_p, conv_p, C_p, n_p, m_p,
            ckv_s, krope_s, conv_s, C_s, n_s, m_s)
```

```python
import functools

import jax
import jax.numpy as jnp
from jax import lax
from jax.experimental import pallas as pl
from jax.experimental.pallas import tpu as pltpu

F32 = jnp.float32
BF16 = jnp.bfloat16

LANES = 128
HEAD_DIM = 64
CONV_WIDTH = 256
CONV_K = 3
MLA_HEADS = 8
Q_RANK = 256
KV_RANK = 128
QK_NOPE = 64
QK_ROPE = 32
V_DIM = 64
ROPE_HALF = QK_ROPE // 2
ROPE_THETA = 10000.0
MLSTM_HEADS = 4
MLSTM_DH = 64
MLSTM_WIDTH = MLSTM_HEADS * MLSTM_DH
N_GROUPS = 4
EXPERTS_PER_GROUP = 8
N_EXPERTS = N_GROUPS * EXPERTS_PER_GROUP
TOP_K = 2
D_EXPERT = 256
LN_EPS = 1e-5
RMS_EPS = 1e-6
QK_SCALE = (QK_NOPE + QK_ROPE) ** -0.5
LOG2_E = 1.4426950408889634
Q_FOLD = QK_SCALE * LOG2_E

Z_WIDTH = 3 * CONV_WIDTH + Q_RANK + KV_RANK + 4 * MLSTM_WIDTH + LANES
GATE_LANE = QK_ROPE
ROUTER_LANE = N_GROUPS

VMEM_LIMIT = 56 * 1024 * 1024

TOKEN_TILE = 512
ATTN_TILE = 512
MLSTM_CHUNK = 256
COMBINE_TILE = 256
PROMPT_EXPERT_BLOCK = 256
SAMPLE_EXPERT_BLOCK = 16
PAGES_PER_CHUNK = 64


def _cparams(n_grid, row_dma=False):
    return pltpu.CompilerParams(dimension_semantics=("arbitrary",) * n_grid, vmem_limit_bytes=VMEM_LIMIT,
                                disable_bounds_checks=row_dma)


def _layer_norm(x, g, b):
    mu = jnp.mean(x, -1, keepdims=True)
    xc = x - mu
    var = jnp.mean(xc * xc, -1, keepdims=True)
    return xc * lax.rsqrt(var + LN_EPS) * g + b


def _rms_norm(x, g):
    return x * lax.rsqrt(jnp.mean(x * x, -1, keepdims=True) + RMS_EPS) * g


def _log_sigmoid(x):
    return jnp.minimum(x, 0.0) - jnp.log1p(jnp.exp(-jnp.abs(x)))


def _sigmoid(x):
    return 1.0 / (1.0 + jnp.exp(-x))


def _rope_block(x, cos_t, sin_t, first_lane):
    lane = lax.broadcasted_iota(jnp.int32, x.shape, 1)
    in_first = (lane >= first_lane) & (lane < first_lane + ROPE_HALF)
    partner = jnp.where(in_first, pltpu.roll(x, LANES - ROPE_HALF, 1), pltpu.roll(x, ROPE_HALF, 1))
    return x * cos_t + partner * sin_t


def _in_proj_kernel(*refs, apply_ln0, seq_mode, emit_kv, tm):
    it = iter(refs)
    x_ref = next(it)
    if apply_ln0:
        g0_ref, b0_ref = next(it), next(it)
    w_in_ref, convw_ref = next(it), next(it)
    if not seq_mode:
        hist0_ref, hist1_ref = next(it), next(it)
    qg_ref, wuq_ref, kvg_ref = next(it), next(it), next(it)
    if emit_kv:
        wuk_ref, wuv_ref = next(it), next(it)
    gb_ref, cq_ref, sq_ref, ck_ref, sk_ref = next(it), next(it), next(it), next(it), next(it)
    if apply_ln0:
        xn_ref = next(it)
    yconv_ref, q_ref, ckv_ref, krope_ref = next(it), next(it), next(it), next(it)
    mq_ref, mkT_ref, mv_ref, og_ref, gates_ref, gatesT_ref = (next(it) for _ in range(6))
    if emit_kv:
        k_ref, v_ref = next(it), next(it)
    if seq_mode:
        cstate_ref = next(it)
        ubuf = next(it)
    else:
        u_ref = next(it)

    lane = lax.broadcasted_iota(jnp.int32, (tm, LANES), 1)
    x = x_ref[...]
    if apply_ln0:
        x = _layer_norm(x, g0_ref[...], b0_ref[...])
        xn_ref[...] = x
    z = jnp.dot(x.astype(BF16), w_in_ref[...], preferred_element_type=F32)
    o = 0
    cb = z[:, o:o + CONV_WIDTH]; o += CONV_WIDTH
    cc = z[:, o:o + CONV_WIDTH]; o += CONV_WIDTH
    ch = z[:, o:o + CONV_WIDTH]; o += CONV_WIDTH
    c_q = z[:, o:o + Q_RANK]; o += Q_RANK
    c_kv = z[:, o:o + KV_RANK]; o += KV_RANK
    mq = z[:, o:o + MLSTM_WIDTH]; o += MLSTM_WIDTH
    mk = z[:, o:o + MLSTM_WIDTH]; o += MLSTM_WIDTH
    mv = z[:, o:o + MLSTM_WIDTH]; o += MLSTM_WIDTH
    mo = z[:, o:o + MLSTM_WIDTH]; o += MLSTM_WIDTH
    last = z[:, o:o + LANES]

    u = cc * ch
    w = convw_ref[...]
    if seq_mode:
        s_idx = pl.program_id(1)

        @pl.when(s_idx == 0)
        def _():
            ubuf[0:8, :] = jnp.zeros((8, CONV_WIDTH), F32)

        ubuf[8:8 + tm, :] = u
        conv = ubuf[6:6 + tm, :] * w[0:1] + ubuf[7:7 + tm, :] * w[1:2] + u * w[2:3]
        ubuf[0:8, :] = ubuf[tm:tm + 8, :]

        @pl.when(s_idx == pl.num_programs(1) - 1)
        def _():
            cstate_ref[...] = ubuf[6:8, :]
    else:
        conv = hist0_ref[...] * w[0:1] + hist1_ref[...] * w[1:2] + u * w[2:3]
        u_ref[...] = u
    yconv_ref[...] = cb * conv

    cqn = _rms_norm(c_q, qg_ref[...])
    q = jnp.dot(cqn.astype(BF16), wuq_ref[...], preferred_element_type=F32)
    cq, sq = cq_ref[...], sq_ref[...]
    for h in range(MLA_HEADS):
        qh = _rope_block(q[:, h * LANES:(h + 1) * LANES], cq, sq, QK_NOPE) * Q_FOLD
        q_ref[:, h * LANES:(h + 1) * LANES] = qh.astype(q_ref.dtype)

    ckv = _rms_norm(c_kv, kvg_ref[...])
    ckv_ref[...] = ckv
    kr = _rope_block(last, ck_ref[...], sk_ref[...], 0)
    krope_ref[...] = kr[:, 0:QK_ROPE]
    if emit_kv:
        ckv_b = ckv.astype(BF16)
        kfull = jnp.dot(ckv_b, wuk_ref[...], preferred_element_type=F32)
        kr_shift = pltpu.roll(kr, QK_NOPE, 1)
        for h in range(MLA_HEADS):
            k_ref[:, h * LANES:(h + 1) * LANES] = (kfull[:, h * LANES:(h + 1) * LANES] + kr_shift).astype(BF16)
        vfull = jnp.dot(ckv_b, wuv_ref[...], preferred_element_type=F32)
        ones_hi = jnp.where(lane >= V_DIM, 1.0, 0.0)
        for h in range(MLA_HEADS):
            v_ref[:, h * LANES:(h + 1) * LANES] = (vfull[:, h * LANES:(h + 1) * LANES] + ones_hi).astype(BF16)

    mq_ref[...] = mq.astype(mq_ref.dtype)
    mkT_ref[...] = (mk * (MLSTM_DH ** -0.5)).T.astype(mkT_ref.dtype)
    mv_ref[...] = mv.astype(mv_ref.dtype)
    og_ref[...] = mo
    g = last + gb_ref[...]
    is_forget = (lane >= GATE_LANE + MLSTM_HEADS) & (lane < GATE_LANE + 2 * MLSTM_HEADS)
    g = jnp.where(is_forget, _log_sigmoid(g), g)
    gates_ref[...] = g[:, GATE_LANE:GATE_LANE + 2 * MLSTM_HEADS]
    gatesT_ref[...] = g.T[GATE_LANE:GATE_LANE + 2 * MLSTM_HEADS, :]


def _in_proj(x, lw, tabs, *, apply_ln0, ln0, seq_mode, emit_kv, hist=None):
    B, S, D = x.shape
    tm = min(TOKEN_TILE, S)
    assert S % tm == 0
    ns = S // tm
    tok = lambda w: pl.BlockSpec((None, tm, w), lambda b, s: (b, s, 0))
    full2 = lambda a: pl.BlockSpec(a.shape, lambda b, s: (0, 0))
    tab = pl.BlockSpec((tm, LANES), lambda b, s: (s, 0))

    ins, specs = [x], [tok(D)]
    if apply_ln0:
        ins += [ln0[0], ln0[1]]; specs += [full2(ln0[0]), full2(ln0[1])]
    ins += [lw["w_in"], lw["conv_w"]]; specs += [full2(lw["w_in"]), full2(lw["conv_w"])]
    if not seq_mode:
        ins += [hist[0], hist[1]]; specs += [tok(CONV_WIDTH), tok(CONV_WIDTH)]
    ins += [lw["q_norm_g"], lw["w_uq"], lw["kv_norm_g"]]
    specs += [full2(lw["q_norm_g"]), full2(lw["w_uq"]), full2(lw["kv_norm_g"])]
    if emit_kv:
        ins += [lw["w_uk"], lw["w_uv"]]; specs += [full2(lw["w_uk"]), full2(lw["w_uv"])]
    ins += [lw["gate_b"]] + list(tabs)
    specs += [full2(lw["gate_b"])] + [tab] * 4

    outs, ospecs = [], []

    def add(shape, dtype, spec):
        outs.append(jax.ShapeDtypeStruct(shape, dtype)); ospecs.append(spec)

    if apply_ln0:
        add((B, S, D), F32, tok(D))
    add((B, S, CONV_WIDTH), F32, tok(CONV_WIDTH))
    add((B, S, MLA_HEADS * LANES), BF16 if emit_kv else F32, tok(MLA_HEADS * LANES))
    add((B, S, KV_RANK), F32, tok(KV_RANK))
    add((B, S, QK_ROPE), F32, tok(QK_ROPE))
    mdt = BF16 if seq_mode else F32
    add((B, S, MLSTM_WIDTH), mdt, tok(MLSTM_WIDTH))
    add((B, MLSTM_WIDTH, S), mdt, pl.BlockSpec((None, MLSTM_WIDTH, tm), lambda b, s: (b, 0, s)))
    add((B, S, MLSTM_WIDTH), mdt, tok(MLSTM_WIDTH))
    add((B, S, MLSTM_WIDTH), F32, tok(MLSTM_WIDTH))
    add((B, S, 2 * MLSTM_HEADS), F32, tok(2 * MLSTM_HEADS))
    add((B, 2 * MLSTM_HEADS, S), F32, pl.BlockSpec((None, 2 * MLSTM_HEADS, tm), lambda b, s: (b, 0, s)))
    if emit_kv:
        add((B, S, MLA_HEADS * LANES), BF16, tok(MLA_HEADS * LANES))
        add((B, S, MLA_HEADS * LANES), BF16, tok(MLA_HEADS * LANES))
    scratch = []
    if seq_mode:
        add((B, CONV_K - 1, CONV_WIDTH), F32, pl.BlockSpec((None, CONV_K - 1, CONV_WIDTH), lambda b, s: (b, 0, 0)))
        scratch = [pltpu.VMEM((tm + 8, CONV_WIDTH), F32)]
    else:
        add((B, S, CONV_WIDTH), F32, tok(CONV_WIDTH))
    kern = functools.partial(_in_proj_kernel, apply_ln0=apply_ln0, seq_mode=seq_mode, emit_kv=emit_kv, tm=tm)
    return pl.pallas_call(kern, grid=(B, ns), in_specs=specs, out_specs=ospecs, out_shape=outs,
                          scratch_shapes=scratch, compiler_params=_cparams(2), name="in_proj")(*ins)


def _flash_attn_kernel(q_ref, k_ref, v_ref, o_ref, *, tq):
    qi = pl.program_id(2)
    row = lax.broadcasted_iota(jnp.int32, (tq, tq), 0)
    col = lax.broadcasted_iota(jnp.int32, (tq, tq), 1)
    causal = col <= row
    def block(ki, carry, masked):
        start = pl.multiple_of(ki * tq, tq)
        new = []
        for hh in range(2):
            m, acc = carry[hh]
            hl = slice(hh * LANES, (hh + 1) * LANES)
            k = k_ref[pl.ds(start, tq), hl]
            v = v_ref[pl.ds(start, tq), hl]
            s = lax.dot_general(q_ref[:, hl], k, (((1,), (1,)), ((), ())), preferred_element_type=F32)
            if masked:
                s = jnp.where(causal, s, -jnp.inf)
            m_new = jnp.maximum(m, jnp.max(s, -1, keepdims=True))
            p = jnp.exp2(s - m_new)
            acc = jnp.exp2(m - m_new) * acc + jnp.dot(p.astype(BF16), v, preferred_element_type=F32)
            new.append((m_new, acc))
        return tuple(new)

    init = ((jnp.full((tq, 1), -jnp.inf, F32), jnp.zeros((tq, LANES), F32)),) * 2
    def two_blocks(j, carry):
        return block(2 * j + 1, block(2 * j, carry, masked=False), masked=False)

    carry = lax.fori_loop(0, qi // 2, two_blocks, init)
    carry = lax.cond(qi % 2 == 1, lambda c: block(qi - 1, c, masked=False), lambda c: c, carry)
    fin = block(qi, carry, masked=True)
    outs = [acc / pltpu.roll(acc, V_DIM, 1) for _, acc in fin]
    lane = lax.broadcasted_iota(jnp.int32, (tq, LANES), 1)
    o_ref[...] = jnp.where(lane < V_DIM, outs[0], pltpu.roll(outs[1], V_DIM, 1))


def _flash_attn(q, k, v):
    B, S, _ = q.shape
    tq = min(ATTN_TILE, S)
    assert S % tq == 0
    kern = functools.partial(_flash_attn_kernel, tq=tq)
    return pl.pallas_call(
        kern, grid=(B, MLA_HEADS // 2, S // tq),
        in_specs=[pl.BlockSpec((None, tq, 2 * LANES), lambda b, h, i: (b, i, h)),
                  pl.BlockSpec((None, S, 2 * LANES), lambda b, h, i: (b, 0, h)),
                  pl.BlockSpec((None, S, 2 * LANES), lambda b, h, i: (b, 0, h))],
        out_specs=pl.BlockSpec((None, tq, 2 * V_DIM), lambda b, h, i: (b, i, h)),
        out_shape=jax.ShapeDtypeStruct((B, S, MLA_HEADS * V_DIM), F32),
        compiler_params=_cparams(3), name="flash_attn")(q, k, v)


def _paged_attn_kernel(pt_ref, q_ref, ckvn_ref, krn_ref, wukT_ref, wuv_ref, cckv_ref, ckrT_ref, o_ref,
                       kbuf, rbuf, sems, qlat_scr, qr_scr, m_scr, l_scr, acc_scr, *, layer, npc, nchunks, page):
    b, c = pl.program_id(0), pl.program_id(1)
    g = b * nchunks + c
    total = pl.num_programs(0) * nchunks
    slot = g % 2

    def page_copies(bb, cc, sl, j):
        pg = pt_ref[bb, cc * npc + j]
        return (pltpu.make_async_copy(cckv_ref.at[layer, pg], kbuf.at[sl, j], sems.at[0, sl]),
                pltpu.make_async_copy(ckrT_ref.at[layer, pg], rbuf.at[sl, j], sems.at[1, sl]))

    def start_chunk(bb, cc, sl):
        def issue(j, _):
            for cp in page_copies(bb, cc, sl, j):
                cp.start()
            return 0
        lax.fori_loop(0, npc, issue, 0, unroll=4)

    @pl.when(g == 0)
    def _():
        start_chunk(b, c, slot)

    @pl.when(g + 1 < total)
    def _():
        nxt = g + 1
        start_chunk(nxt // nchunks, nxt % nchunks, 1 - slot)

    @pl.when(c == 0)
    def _():
        qrow = q_ref[0]
        for h in range(MLA_HEADS):
            qn = jnp.broadcast_to(qrow[:, h * LANES:h * LANES + QK_NOPE], (8, QK_NOPE)).astype(BF16)
            qlat_scr[h:h + 1, :] = jnp.dot(qn, wukT_ref[h], preferred_element_type=F32)[0:1]
            qr_scr[h:h + 1, :] = qrow[:, h * LANES + QK_NOPE:h * LANES + QK_NOPE + QK_ROPE]
        m_scr[...] = jnp.full(m_scr.shape, -jnp.inf, F32)
        l_scr[...] = jnp.zeros(l_scr.shape, F32)
        acc_scr[...] = jnp.zeros(acc_scr.shape, F32)

    pltpu.make_async_copy(cckv_ref.at[layer, pl.ds(0, npc)], kbuf.at[slot], sems.at[0, slot]).wait()
    pltpu.make_async_copy(ckrT_ref.at[layer, pl.ds(0, npc)], rbuf.at[slot], sems.at[1, slot]).wait()

    kb = kbuf[slot].reshape(npc * page, KV_RANK).astype(BF16)
    qr = qr_scr[...].astype(BF16)
    s_rope = jnp.concatenate([jnp.dot(qr, rbuf[slot, j].astype(BF16), preferred_element_type=F32)
                              for j in range(npc)], axis=-1)
    s = lax.dot_general(qlat_scr[...].astype(BF16), kb, (((1,), (1,)), ((), ())),
                        preferred_element_type=F32) + s_rope
    m = m_scr[...]
    m_new = jnp.maximum(m, jnp.max(s, -1, keepdims=True))
    alpha = jnp.exp2(m - m_new)
    p = jnp.exp2(s - m_new)
    l_scr[...] = alpha * l_scr[...] + jnp.sum(p, -1, keepdims=True)
    acc_scr[...] = alpha * acc_scr[...] + jnp.dot(p.astype(BF16), kb, preferred_element_type=F32)
    m_scr[...] = m_new

    @pl.when(c == nchunks - 1)
    def _():
        ckv_new, kr_new = ckvn_ref[0], krn_ref[0]
        s_new = (jnp.sum(qlat_scr[...] * ckv_new, -1, keepdims=True)
                 + jnp.sum(qr_scr[...] * kr_new, -1, keepdims=True))
        m_old = m_scr[...]
        m_fin = jnp.maximum(m_old, s_new)
        a = jnp.exp2(m_old - m_fin)
        p_new = jnp.exp2(s_new - m_fin)
        l_fin = a * l_scr[...] + p_new
        o_lat = (a * acc_scr[...] + p_new * ckv_new) / l_fin
        for h in range(MLA_HEADS):
            oh = jnp.broadcast_to(o_lat[h:h + 1, :], (8, KV_RANK)).astype(BF16)
            o_ref[0, :, h * V_DIM:(h + 1) * V_DIM] = jnp.dot(oh, wuv_ref[h], preferred_element_type=F32)[0:1]


def _paged_attn(q, ckv_new, kr_new, wukT, wuv_h, cache_ckv, cache_krope_t, page_table, layer):
    DB = q.shape[0]
    n_pages = page_table.shape[1]
    page = cache_ckv.shape[2]
    npc = min(PAGES_PER_CHUNK, n_pages)
    assert n_pages % npc == 0
    nchunks = n_pages // npc
    kern = functools.partial(_paged_attn_kernel, layer=layer, npc=npc, nchunks=nchunks, page=page)
    row = lambda w: pl.BlockSpec((1, 1, w), lambda b, c, pt: (b, 0, 0))
    full3 = lambda a: pl.BlockSpec(a.shape, lambda b, c, pt: (0, 0, 0))
    gs = pltpu.PrefetchScalarGridSpec(
        num_scalar_prefetch=1, grid=(DB, nchunks),
        in_specs=[row(MLA_HEADS * LANES), row(KV_RANK), row(QK_ROPE), full3(wukT), full3(wuv_h),
                  pl.BlockSpec(memory_space=pl.ANY), pl.BlockSpec(memory_space=pl.ANY)],
        out_specs=row(MLA_HEADS * V_DIM),
        scratch_shapes=[pltpu.VMEM((2, npc, page, KV_RANK), F32), pltpu.VMEM((2, npc, QK_ROPE, page), F32),
                        pltpu.SemaphoreType.DMA((2, 2)),
                        pltpu.VMEM((MLA_HEADS, KV_RANK), F32), pltpu.VMEM((MLA_HEADS, QK_ROPE), F32),
                        pltpu.VMEM((MLA_HEADS, 1), F32), pltpu.VMEM((MLA_HEADS, 1), F32),
                        pltpu.VMEM((MLA_HEADS, KV_RANK), F32)])
    return pl.pallas_call(kern, grid_spec=gs, out_shape=jax.ShapeDtypeStruct((DB, 1, MLA_HEADS * V_DIM), F32),
                          compiler_params=_cparams(2, row_dma=True), name="paged_attn")(
        page_table, q, ckv_new, kr_new, wukT, wuv_h, cache_ckv, cache_krope_t)


def _mlstm_seq_kernel(q_ref, kT_ref, v_ref, g_ref, gT_ref, h_ref, ct_out, n_out, m_out, ct_scr, n_scr, m_scr, *, L):
    c = pl.program_id(1)

    @pl.when(c == 0)
    def _():
        ct_scr[...] = jnp.zeros(ct_scr.shape, F32)
        n_scr[...] = jnp.zeros(n_scr.shape, F32)
        m_scr[...] = jnp.zeros(m_scr.shape, F32)

    g = g_ref[...]
    gT = gT_ref[...]
    row = lax.broadcasted_iota(jnp.int32, (L, L), 0)
    col = lax.broadcasted_iota(jnp.int32, (L, L), 1)
    causal = col <= row
    lower = causal.astype(F32)
    upper = (row <= col).astype(F32)
    b_cols = jnp.dot(lower, g, precision=lax.Precision.HIGHEST, preferred_element_type=F32)
    b_rows = jnp.dot(gT, upper, precision=lax.Precision.HIGHEST, preferred_element_type=F32)
    for h in range(MLSTM_HEADS):
        sl = slice(h * MLSTM_DH, (h + 1) * MLSTM_DH)
        qh, khT, vh = q_ref[:, sl], kT_ref[sl, :], v_ref[:, sl]
        b_col = b_cols[:, MLSTM_HEADS + h:MLSTM_HEADS + h + 1]
        b_row = b_rows[MLSTM_HEADS + h:MLSTM_HEADS + h + 1, :]
        i_row, i_col = gT[h:h + 1, :], g[:, h:h + 1]
        m_prev = m_scr[h:h + 1, 0:1]
        log_d = jnp.where(causal, b_col - b_row + i_row, -jnp.inf)
        m_inter = b_col + m_prev
        m_t = jnp.maximum(m_inter, jnp.max(log_d, -1, keepdims=True))
        d = jnp.exp(log_d - m_t)
        w_inter = jnp.exp(m_inter - m_t)
        s = jnp.dot(qh, khT, preferred_element_type=F32) * d
        ct = ct_scr[h]
        n = n_scr[h:h + 1, :]
        num = (jnp.dot(s.astype(BF16), vh, preferred_element_type=F32)
               + w_inter * jnp.dot(qh, ct.astype(BF16), preferred_element_type=F32))
        den = jnp.sum(s, -1, keepdims=True) + w_inter * jnp.sum(qh.astype(F32) * n, -1, keepdims=True)
        h_ref[:, sl] = num / jnp.maximum(jnp.abs(den), jnp.exp(-m_t))
        m_new = m_t[L - 1:L, :]
        b_last = b_col[L - 1:L, :]
        w_old = jnp.exp(b_last + m_prev - m_new)
        w_s = jnp.exp(b_last - b_col + i_col - m_new)
        ct_scr[h] = w_old * ct + jnp.dot(khT, (w_s * vh.astype(F32)).astype(BF16), preferred_element_type=F32)
        w_s_row = jnp.exp(b_last - b_row + i_row - m_new)
        k_sum = jnp.sum(khT.astype(F32) * w_s_row, -1, keepdims=True)
        n_scr[h:h + 1, :] = w_old * n + _col_to_row(k_sum)
        m_scr[h:h + 1, :] = jnp.broadcast_to(m_new, (1, LANES))

    @pl.when(c == pl.num_programs(1) - 1)
    def _():
        ct_out[...] = ct_scr[...]
        n_out[...] = n_scr[...]
        m_out[...] = m_scr[...]


def _eye(n):
    return (lax.broadcasted_iota(jnp.int32, (n, n), 0) == lax.broadcasted_iota(jnp.int32, (n, n), 1)).astype(F32)


def _col_to_row(col):
    return jnp.sum(_eye(col.shape[0]) * col, 0, keepdims=True)


def _row_to_col(row):
    return jnp.sum(_eye(row.shape[1]) * row, 1, keepdims=True)


def _mlstm_seq(mq, mkT, mv, gates, gatesT):
    B, S, _ = mq.shape
    L = min(MLSTM_CHUNK, S)
    assert S % L == 0
    kern = functools.partial(_mlstm_seq_kernel, L=L)
    tok = lambda w: pl.BlockSpec((None, L, w), lambda b, c: (b, c, 0))
    tokT = lambda w: pl.BlockSpec((None, w, L), lambda b, c: (b, 0, c))
    st = lambda *dims: pl.BlockSpec((None,) + dims, lambda b, c: (b,) + (0,) * len(dims))
    return pl.pallas_call(
        kern, grid=(B, S // L),
        in_specs=[tok(MLSTM_WIDTH), tokT(MLSTM_WIDTH), tok(MLSTM_WIDTH), tok(2 * MLSTM_HEADS), tokT(2 * MLSTM_HEADS)],
        out_specs=[tok(MLSTM_WIDTH), st(MLSTM_HEADS, MLSTM_DH, MLSTM_DH), st(MLSTM_HEADS, MLSTM_DH), st(8, LANES)],
        out_shape=[jax.ShapeDtypeStruct((B, S, MLSTM_WIDTH), F32),
                   jax.ShapeDtypeStruct((B, MLSTM_HEADS, MLSTM_DH, MLSTM_DH), F32),
                   jax.ShapeDtypeStruct((B, MLSTM_HEADS, MLSTM_DH), F32),
                   jax.ShapeDtypeStruct((B, 8, LANES), F32)],
        scratch_shapes=[pltpu.VMEM((MLSTM_HEADS, MLSTM_DH, MLSTM_DH), F32), pltpu.VMEM((MLSTM_HEADS, MLSTM_DH), F32),
                        pltpu.VMEM((8, LANES), F32)],
        compiler_params=_cparams(2), name="mlstm_seq")(mq, mkT, mv, gates, gatesT)


def _mlstm_step_kernel(q_ref, k_ref, v_ref, g_ref, c_ref, n_ref, m_ref, h_ref, c_out, n_out, m_out):
    g = g_ref[0]
    m_all = m_ref[0]
    for h in range(MLSTM_HEADS):
        sl = slice(h * MLSTM_DH, (h + 1) * MLSTM_DH)
        q, k, v = q_ref[0][:, sl], k_ref[0][:, sl], v_ref[0][:, sl]
        C, n = c_ref[0, h], n_ref[0, h:h + 1, :]
        ig, lf = g[:, h:h + 1], g[:, MLSTM_HEADS + h:MLSTM_HEADS + h + 1]
        m_prev = m_all[:, h:h + 1]
        m_inter = lf + m_prev
        m_t = jnp.maximum(m_inter, ig)
        d = jnp.exp(ig - m_t)
        w_inter = jnp.exp(m_inter - m_t)
        s = jnp.sum(q * k, -1, keepdims=True) * d
        cq = jnp.sum(C * q, -1, keepdims=True)
        v_col = _row_to_col(v)
        num = s * v_col + w_inter * cq
        den = s + w_inter * jnp.sum(n * q, -1, keepdims=True)
        h_col = num / jnp.maximum(jnp.abs(den), jnp.exp(-m_t))
        h_ref[0, :, sl] = _col_to_row(h_col)
        c_out[0, h] = w_inter * C + d * (v_col * k)
        n_out[0, h:h + 1, :] = w_inter * n + d * k
        m_out[0, :, h:h + 1] = m_t


def _mlstm_step(mq, mk, mv, gates, C0, n0, m0):
    DB = mq.shape[0]
    r3 = lambda a: a.reshape(DB, 1, a.shape[-1])
    row = lambda w: pl.BlockSpec((1, 1, w), lambda b: (b, 0, 0))
    cspec = pl.BlockSpec((1, MLSTM_HEADS, MLSTM_DH, MLSTM_DH), lambda b: (b, 0, 0, 0))
    nspec = pl.BlockSpec((1, MLSTM_HEADS, MLSTM_DH), lambda b: (b, 0, 0))
    return pl.pallas_call(
        _mlstm_step_kernel, grid=(DB,),
        in_specs=[row(MLSTM_WIDTH)] * 3 + [row(2 * MLSTM_HEADS), cspec, nspec, row(MLSTM_HEADS)],
        out_specs=[row(MLSTM_WIDTH), cspec, nspec, row(MLSTM_HEADS)],
        out_shape=[jax.ShapeDtypeStruct((DB, 1, MLSTM_WIDTH), F32), jax.ShapeDtypeStruct(C0.shape, F32),
                   jax.ShapeDtypeStruct(n0.shape, F32), jax.ShapeDtypeStruct((DB, 1, MLSTM_HEADS), F32)],
        compiler_params=_cparams(1), name="mlstm_step")(r3(mq), r3(mk), r3(mv), r3(gates), C0, n0, r3(m0))


def _mix_out_kernel(yc_ref, att_ref, hm_ref, og_ref, x_ref, mg_ref, wout_ref, g1_ref, b1_ref, wr_ref, br_ref,
                    h1_ref, route_ref, cnt_ref, ymix_scr, carry_scr, *, tm, alpha):
    @pl.when(pl.program_id(0) == 0)
    def _():
        carry_scr[...] = jnp.zeros(carry_scr.shape, F32)

    lane = lax.broadcasted_iota(jnp.int32, (tm, LANES), 1)
    low = lane < HEAD_DIM

    def norm_store(y, j):
        y2 = y * y
        s_lo = jnp.sum(jnp.where(low, y2, 0.0), -1, keepdims=True)
        s_hi = jnp.sum(jnp.where(low, 0.0, y2), -1, keepdims=True)
        inv = jnp.where(low, lax.rsqrt(s_lo / HEAD_DIM + RMS_EPS), lax.rsqrt(s_hi / HEAD_DIM + RMS_EPS))
        ymix_scr[:, j * LANES:(j + 1) * LANES] = (y * inv * mg_ref[:, j * LANES:(j + 1) * LANES]).astype(BF16)

    j = 0
    for src, width in ((yc_ref, CONV_WIDTH), (att_ref, MLA_HEADS * V_DIM)):
        for t in range(width // LANES):
            norm_store(src[:, t * LANES:(t + 1) * LANES], j)
            j += 1
    for t in range(MLSTM_WIDTH // LANES):
        sl = slice(t * LANES, (t + 1) * LANES)
        norm_store(_sigmoid(og_ref[:, sl]) * hm_ref[:, sl], j)
        j += 1

    proj = jnp.dot(ymix_scr[...], wout_ref[...], preferred_element_type=F32)
    h1 = _layer_norm(alpha * x_ref[...] + proj, g1_ref[...], b1_ref[...])
    h1_ref[...] = h1

    logits = jnp.dot(h1, wr_ref[...], precision=lax.Precision.HIGHEST, preferred_element_type=F32) + br_ref[...]
    lanef = lane.astype(F32)
    big = float(LANES)
    neg = -jnp.inf
    is_g = lane < N_GROUPS
    gl = jnp.where(is_g, logits, neg)
    gmax = jnp.max(gl, -1, keepdims=True)
    g_sel = jnp.min(jnp.where(gl == gmax, lanef, big), -1, keepdims=True)
    g_w = 1.0 / jnp.sum(jnp.where(is_g, jnp.exp(logits - gmax), 0.0), -1, keepdims=True)
    lo = ROUTER_LANE + EXPERTS_PER_GROUP * g_sel
    in_grp = (lanef >= lo) & (lanef < lo + EXPERTS_PER_GROUP)
    el = jnp.where(in_grp, logits, neg)
    e1 = jnp.max(el, -1, keepdims=True)
    i1 = jnp.min(jnp.where(el == e1, lanef, big), -1, keepdims=True)
    el2 = jnp.where(lanef == i1, neg, el)
    e2 = jnp.max(el2, -1, keepdims=True)
    i2 = jnp.min(jnp.where(el2 == e2, lanef, big), -1, keepdims=True)
    zsum = jnp.sum(jnp.where(in_grp, jnp.exp(logits - e1), 0.0), -1, keepdims=True)
    p1 = 1.0 / zsum
    p2 = jnp.exp(e2 - e1) / zsum
    psum = p1 + p2
    w1 = p1 / psum * g_w
    w2 = p2 / psum * g_w

    oh1 = (lanef == i1).astype(F32)
    oh2 = (lanef == i2).astype(F32)
    both = oh1 + oh2
    r = lax.broadcasted_iota(jnp.int32, (tm, tm), 0)
    c = lax.broadcasted_iota(jnp.int32, (tm, tm), 1)
    before = jnp.dot((c < r).astype(BF16), both.astype(BF16), preferred_element_type=F32) + carry_scr[0:1, :]
    r1 = jnp.sum(oh1 * before, -1, keepdims=True)
    r2 = jnp.sum(oh2 * before, -1, keepdims=True)
    carry = carry_scr[0:1, :] + jnp.sum(both, 0, keepdims=True)
    carry_scr[...] = jnp.broadcast_to(carry, carry_scr.shape)
    cnt_ref[...] = jnp.broadcast_to(carry, cnt_ref.shape)

    vals = (i1 - ROUTER_LANE, i2 - ROUTER_LANE, r1, r2, w1, w2)
    packed = jnp.zeros((tm, LANES), F32)
    for idx, val in enumerate(vals):
        packed = jnp.where(lane == idx, val, packed)
    route_ref[...] = packed[:, 0:8]


def _mix_out(yconv, att, hm, og, x, lw, alpha):
    T, D = x.shape
    tm = min(TOKEN_TILE, T)
    assert T % tm == 0
    tok = lambda w: pl.BlockSpec((tm, w), lambda i: (i, 0))
    full = lambda a: pl.BlockSpec(a.shape, lambda i: (0, 0))
    ws = [lw["mix_norm_g"], lw["w_out"], lw["ln1_g"], lw["ln1_b"], lw["w_router"], lw["b_router"]]
    kern = functools.partial(_mix_out_kernel, tm=tm, alpha=alpha)
    return pl.pallas_call(
        kern, grid=(T // tm,),
        in_specs=[tok(CONV_WIDTH), tok(MLA_HEADS * V_DIM), tok(MLSTM_WIDTH), tok(MLSTM_WIDTH), tok(D)] + [full(a) for a in ws],
        out_specs=[tok(D), tok(8), pl.BlockSpec((8, LANES), lambda i: (0, 0))],
        out_shape=[jax.ShapeDtypeStruct((T, D), F32), jax.ShapeDtypeStruct((T, 8), F32),
                   jax.ShapeDtypeStruct((8, LANES), F32)],
        scratch_shapes=[pltpu.VMEM((tm, D), BF16), pltpu.VMEM((8, LANES), F32)],
        compiler_params=_cparams(1), name="mix_out")(yconv, att, hm, og, x, *ws)


def _ple_dispatch_kernel(dest_ref, h1_ref, p_ref, wpe_ref, wpg_ref, xs_in, ple_ref, xs_out, sem, *, tm):
    del xs_in

    def issue(r, _):
        for k in range(TOP_K):
            pltpu.make_async_copy(h1_ref.at[pl.ds(r, 1)], xs_out.at[pl.ds(dest_ref[TOP_K * r + k], 1)], sem).start()
        return 0
    lax.fori_loop(0, tm, issue, 0, unroll=8)

    emb = jnp.dot(p_ref[...].astype(BF16), wpe_ref[...], preferred_element_type=F32)
    gate = jnp.dot(h1_ref[...].astype(BF16), wpg_ref[...], preferred_element_type=F32)
    ple_ref[...] = emb * _sigmoid(gate)

    for k in range(TOP_K):
        pltpu.make_async_copy(h1_ref, xs_out.at[pl.ds(0, tm)], sem).wait()


def _ple_dispatch(dest_flat, h1, p, lw, n_rows):
    T, D = h1.shape
    tm = min(TOKEN_TILE, T)
    tok = lambda w: pl.BlockSpec((tm, w), lambda i: (i, 0))
    full = lambda a: pl.BlockSpec(a.shape, lambda i: (0, 0))
    xs0 = jnp.zeros((n_rows, D), F32)
    kern = functools.partial(_ple_dispatch_kernel, tm=tm)
    ple, xs = pl.pallas_call(
        kern, grid=(T // tm,),
        in_specs=[pl.BlockSpec((TOP_K * tm,), lambda i: (i,), memory_space=pltpu.SMEM),
                  tok(D), tok(p.shape[1]), full(lw["w_pe"]), full(lw["w_pg"]), pl.BlockSpec(memory_space=pl.ANY)],
        out_specs=[tok(D), pl.BlockSpec(memory_space=pl.ANY)],
        out_shape=[jax.ShapeDtypeStruct((T, D), F32), jax.ShapeDtypeStruct((n_rows, D), F32)],
        scratch_shapes=[pltpu.SemaphoreType.DMA(())],
        input_output_aliases={5: 1},
        compiler_params=_cparams(1, row_dma=True), name="ple_dispatch")(dest_flat, h1, p, lw["w_pe"], lw["w_pg"], xs0)
    return ple, xs


def _experts_kernel(be_ref, nu_ref, xs_ref, wgu_ref, wd_ref, ys_ref):
    del be_ref
    i = pl.program_id(0)

    @pl.when(i < nu_ref[0])
    def _():
        gu = jnp.dot(xs_ref[...].astype(BF16), wgu_ref[0], preferred_element_type=F32)
        gt, up = gu[:, 0:D_EXPERT], gu[:, D_EXPERT:2 * D_EXPERT]
        hb = gt * _sigmoid(gt) * up
        ys_ref[...] = jnp.dot(hb.astype(BF16), wd_ref[0], preferred_element_type=F32)

    @pl.when(i >= nu_ref[0])
    def _():
        ys_ref[...] = jnp.zeros(ys_ref.shape, F32)


def _experts(blk_expert, n_used, xs, lw, blk):
    n_rows, D = xs.shape
    gs = pltpu.PrefetchScalarGridSpec(
        num_scalar_prefetch=2, grid=(n_rows // blk,),
        in_specs=[pl.BlockSpec((blk, D), lambda i, be, nu: (i, 0)),
                  pl.BlockSpec((1, D, 2 * D_EXPERT), lambda i, be, nu: (be[i], 0, 0)),
                  pl.BlockSpec((1, D_EXPERT, D), lambda i, be, nu: (be[i], 0, 0))],
        out_specs=pl.BlockSpec((blk, D), lambda i, be, nu: (i, 0)))
    return pl.pallas_call(_experts_kernel, grid_spec=gs, out_shape=jax.ShapeDtypeStruct((n_rows, D), F32),
                          compiler_params=_cparams(1), name="experts")(blk_expert, n_used, xs, lw["e_gu"], lw["e_down"])


def _combine_kernel(dest_ref, h1_ref, ple_ref, route_ref, g2_ref, b2_ref, ys_hbm, out_ref, gbuf, sem, *, tm, alpha):
    def issue(r, _):
        for k in range(TOP_K):
            pltpu.make_async_copy(ys_hbm.at[pl.ds(dest_ref[TOP_K * r + k], 1)], gbuf.at[k, pl.ds(r, 1)], sem).start()
        return 0
    lax.fori_loop(0, tm, issue, 0, unroll=8)

    for k in range(TOP_K):
        pltpu.make_async_copy(ys_hbm.at[pl.ds(0, tm)], gbuf.at[k], sem).wait()

    route = route_ref[...]
    moe = gbuf[0] * route[:, 4:5] + gbuf[1] * route[:, 5:6]
    out_ref[...] = _layer_norm(alpha * h1_ref[...] + moe + ple_ref[...], g2_ref[...], b2_ref[...])


def _combine(dest_flat, h1, ple, route, ys, lw, alpha):
    T, D = h1.shape
    tm = min(COMBINE_TILE, T)
    tok = lambda w: pl.BlockSpec((tm, w), lambda i: (i, 0))
    full = lambda a: pl.BlockSpec(a.shape, lambda i: (0, 0))
    kern = functools.partial(_combine_kernel, tm=tm, alpha=alpha)
    return pl.pallas_call(
        kern, grid=(T // tm,),
        in_specs=[pl.BlockSpec((TOP_K * tm,), lambda i: (i,), memory_space=pltpu.SMEM),
                  tok(D), tok(D), tok(8), full(lw["ln2_g"]), full(lw["ln2_b"]), pl.BlockSpec(memory_space=pl.ANY)],
        out_specs=tok(D), out_shape=jax.ShapeDtypeStruct((T, D), F32),
        scratch_shapes=[pltpu.VMEM((TOP_K, tm, D), F32), pltpu.SemaphoreType.DMA(())],
        compiler_params=_cparams(1, row_dma=True), name="combine")(dest_flat, h1, ple, route, lw["ln2_g"], lw["ln2_b"], ys)


def _channel_mixer(yconv, att, hm, og, x, p, lw, alpha, blk):
    T, D = x.shape
    h1, route, cnt = _mix_out(yconv, att, hm, og, x, lw, alpha)
    counts = cnt[0, ROUTER_LANE:ROUTER_LANE + N_EXPERTS].astype(jnp.int32)
    padded = (counts + blk - 1) // blk * blk
    pad_end = jnp.cumsum(padded)
    pad_start = pad_end - padded
    n_blocks = -(-T * TOP_K // blk) + N_EXPERTS
    eid = route[:, 0:TOP_K].astype(jnp.int32)
    rank = route[:, TOP_K:2 * TOP_K].astype(jnp.int32)
    dest_flat = (pad_start[eid] + rank).reshape(T * TOP_K)
    blk_first_row = jnp.arange(n_blocks, dtype=jnp.int32) * blk
    blk_expert = jnp.minimum(jnp.sum((pad_end[None, :] <= blk_first_row[:, None]).astype(jnp.int32), -1), N_EXPERTS - 1)
    n_used = (pad_end[-1:] // blk).astype(jnp.int32)
    ple, xs = _ple_dispatch(dest_flat, h1, p, lw, n_blocks * blk)
    ys = _experts(blk_expert, n_used, xs, lw, blk)
    return _combine(dest_flat, h1, ple, route, ys, lw, alpha)


def _rope_tables(pos, n_rows):
    inv = ROPE_THETA ** (-jnp.arange(ROPE_HALF, dtype=F32) / ROPE_HALF)
    ang = pos.astype(F32)[:, None] * inv
    cos, sin = jnp.cos(ang), jnp.sin(ang)
    if cos.shape[0] != n_rows:
        cos, sin = jnp.broadcast_to(cos, (n_rows, ROPE_HALF)), jnp.broadcast_to(sin, (n_rows, ROPE_HALF))
    z = lambda w: jnp.zeros((n_rows, w), F32)
    one = jnp.ones((n_rows, QK_NOPE), F32)
    pad_q = LANES - QK_NOPE - QK_ROPE
    cq = jnp.concatenate([one, cos, cos, z(pad_q)], -1)
    sq = jnp.concatenate([z(QK_NOPE), -sin, sin, z(pad_q)], -1)
    ck = jnp.concatenate([cos, cos, z(LANES - QK_ROPE)], -1)
    sk = jnp.concatenate([-sin, sin, z(LANES - QK_ROPE)], -1)
    return cq, sq, ck, sk


def _prep_layer(i, w_in, conv_w, q_norm_g, w_uq, kv_norm_g, w_ukv, mlstm_gate_b, mix_norm_g, w_out, ln1_g, ln1_b,
                w_group, b_group, w_expert, b_expert, e_gate, e_up, e_down, w_pe, w_pg, ln2_g, ln2_b):
    D = w_in.shape[1]
    cuts, o = [], 0
    for wdt in [CONV_WIDTH] * 3 + [Q_RANK, KV_RANK, QK_ROPE] + [MLSTM_WIDTH] * 4 + [MLSTM_HEADS] * 2:
        cuts.append((o, o + wdt)); o += wdt
    wi = w_in[i]
    col = lambda j: wi[:, cuts[j][0]:cuts[j][1]]
    tail_pad = jnp.zeros((D, LANES - QK_ROPE - 2 * MLSTM_HEADS), F32)
    w_in_p = jnp.concatenate([col(0), col(1), col(2), col(3), col(4), col(6), col(7), col(8), col(9),
                              col(5), col(10), col(11), tail_pad], -1).astype(BF16)
    wq = w_uq[i].reshape(Q_RANK, MLA_HEADS, QK_NOPE + QK_ROPE)
    w_uq_p = jnp.pad(wq, ((0, 0), (0, 0), (0, LANES - QK_NOPE - QK_ROPE))).reshape(Q_RANK, MLA_HEADS * LANES).astype(BF16)
    wkv = w_ukv[i].reshape(KV_RANK, MLA_HEADS, QK_NOPE + V_DIM)
    w_uk, w_uv = wkv[..., :QK_NOPE], wkv[..., QK_NOPE:]
    w_uk_p = jnp.pad(w_uk, ((0, 0), (0, 0), (0, LANES - QK_NOPE))).reshape(KV_RANK, MLA_HEADS * LANES).astype(BF16)
    gate_b = jnp.concatenate([jnp.zeros((QK_ROPE,), F32), mlstm_gate_b[i],
                              jnp.zeros((LANES - QK_ROPE - 2 * MLSTM_HEADS,), F32)])[None, :]
    w_router = jnp.concatenate([w_group[i], w_expert[i], jnp.zeros((D, LANES - N_GROUPS - N_EXPERTS), F32)], -1)
    b_router = jnp.concatenate([b_group[i], b_expert[i], jnp.zeros((LANES - N_GROUPS - N_EXPERTS,), F32)])[None, :]
    return dict(
        w_in=w_in_p, conv_w=conv_w[i], q_norm_g=q_norm_g[i][None, :], w_uq=w_uq_p, kv_norm_g=kv_norm_g[i][None, :],
        w_uk=w_uk_p,
        w_uv=jnp.pad(w_uv, ((0, 0), (0, 0), (0, LANES - V_DIM))).reshape(KV_RANK, MLA_HEADS * LANES).astype(BF16),
        w_ukT_h=jnp.transpose(w_uk, (1, 2, 0)).astype(BF16),
        w_uv_h=jnp.transpose(w_uv, (1, 0, 2)).astype(BF16),
        gate_b=gate_b, mix_norm_g=mix_norm_g[i][None, :], w_out=w_out[i].astype(BF16),
        ln1_g=ln1_g[i][None, :], ln1_b=ln1_b[i][None, :], w_router=w_router, b_router=b_router,
        e_gu=jnp.concatenate([e_gate[i], e_up[i]], -1).astype(BF16), e_down=e_down[i].astype(BF16),
        w_pe=w_pe[i].astype(BF16), w_pg=w_pg[i].astype(BF16), ln2_g=ln2_g[i][None, :], ln2_b=ln2_b[i][None, :])


def kernel(x_prompt, x_sample, cache_ckv, cache_krope, state_conv, state_mlstm_C, state_mlstm_n, state_mlstm_m,
           page_table, p_prompt, p_sample, ln0_g, ln0_b, w_in, conv_w, q_norm_g, w_uq, kv_norm_g, w_ukv,
           mlstm_gate_b, mix_norm_g, w_out, ln1_g, ln1_b, w_group, b_group, w_expert, b_expert,
           e_gate, e_up, e_down, w_pe, w_pg, ln2_g, ln2_b):
    B, S, D = x_prompt.shape
    DB, DS, _ = x_sample.shape
    assert DS == 1
    depth = w_in.shape[0]
    alpha = (2 * depth) ** 0.25
    past_len = page_table.shape[1] * cache_ckv.shape[2]
    ln0 = (ln0_g[None, :], ln0_b[None, :])
    tabs_p = _rope_tables(jnp.arange(S), S)
    tabs_s = _rope_tables(jnp.full((1,), past_len), DB)
    cache_krope_t = jnp.swapaxes(cache_krope, 2, 3)

    hp = x_prompt
    hs = x_sample.reshape(1, DB, D)
    st_p, st_s = [], []
    for i in range(depth):
        lw = _prep_layer(i, w_in, conv_w, q_norm_g, w_uq, kv_norm_g, w_ukv, mlstm_gate_b, mix_norm_g, w_out,
                         ln1_g, ln1_b, w_group, b_group, w_expert, b_expert, e_gate, e_up, e_down, w_pe, w_pg,
                         ln2_g, ln2_b)
        first = i == 0
        outs = _in_proj(hp, lw, tabs_p, apply_ln0=first, ln0=ln0, seq_mode=True, emit_kv=True)
        if first:
            hp, outs = outs[0], outs[1:]
        yconv, q, ckv, krope, mq, mkT, mv, og, gates, gatesT, k, v, conv_state = outs
        att = _flash_attn(q, k, v)
        hm, ct, n1, m1 = _mlstm_seq(mq, mkT, mv, gates, gatesT)
        T = B * S
        flat = lambda a: a.reshape(T, a.shape[-1])
        hp = _channel_mixer(flat(yconv), flat(att), flat(hm), flat(og), flat(hp), flat(p_prompt[i]), lw, alpha,
                            PROMPT_EXPERT_BLOCK).reshape(B, S, D)
        st_p.append((ckv, krope, conv_state, jnp.swapaxes(ct, -1, -2), n1, m1[:, :MLSTM_HEADS, 0]))
        hist = (state_conv[i][None, :, 0, :], state_conv[i][None, :, 1, :])
        outs = _in_proj(hs, lw, tabs_s, apply_ln0=first, ln0=ln0, seq_mode=False, emit_kv=False, hist=hist)
        if first:
            hs, outs = outs[0], outs[1:]
        yconv, q, ckv, krope, mq, mkT, mv, og, gates, gatesT, u = outs
        r3 = lambda a: a.reshape(DB, 1, a.shape[-1])
        att = _paged_attn(r3(q), r3(ckv), r3(krope), lw["w_ukT_h"], lw["w_uv_h"], cache_ckv, cache_krope_t,
                          page_table, i)
        mk = jnp.swapaxes(mkT[0], 0, 1)
        hm, C1, n1, m1 = _mlstm_step(mq[0].astype(F32), mk.astype(F32), mv[0].astype(F32), gates[0],
                                     state_mlstm_C[i], state_mlstm_n[i], state_mlstm_m[i])
        hs = _channel_mixer(yconv[0], att.reshape(DB, -1), hm.reshape(DB, -1), og[0], hs[0], p_sample[i].reshape(DB, -1),
                            lw, alpha, SAMPLE_EXPERT_BLOCK).reshape(1, DB, D)
        conv_new = jnp.stack([state_conv[i][:, 1, :], u[0]], axis=1)
        st_s.append((ckv.reshape(DB, 1, -1), krope.reshape(DB, 1, -1), conv_new, C1, n1, m1.reshape(DB, MLSTM_HEADS)))
    ckv_p, krope_p, conv_p, C_p, n_p, m_p = [jnp.stack(a) for a in zip(*st_p)]
    ckv_s, krope_s, conv_s, C_s, n_s, m_s = [jnp.stack(a) for a in zip(*st_s)]
    return (hp, hs.reshape(DB, DS, D), ckv_p, krope_p, conv_p, C_p, n_p, m_p,
            ckv_s, krope_s, conv_s, C_s, n_s, m_s)
```

```python
import functools

import jax
import jax.numpy as jnp
from jax import lax
from jax.experimental import pallas as pl
from jax.experimental.pallas import tpu as pltpu

F32 = jnp.float32
BF16 = jnp.bfloat16

LANES = 128
HEAD_DIM = 64
CONV_WIDTH = 256
CONV_K = 3
MLA_HEADS = 8
Q_RANK = 256
KV_RANK = 128
QK_NOPE = 64
QK_ROPE = 32
V_DIM = 64
ROPE_HALF = QK_ROPE // 2
ROPE_THETA = 10000.0
MLSTM_HEADS = 4
MLSTM_DH = 64
MLSTM_WIDTH = MLSTM_HEADS * MLSTM_DH
N_GROUPS = 4
EXPERTS_PER_GROUP = 8
N_EXPERTS = N_GROUPS * EXPERTS_PER_GROUP
TOP_K = 2
D_EXPERT = 256
LN_EPS = 1e-5
RMS_EPS = 1e-6
QK_SCALE = (QK_NOPE + QK_ROPE) ** -0.5
LOG2_E = 1.4426950408889634
Q_FOLD = QK_SCALE * LOG2_E

Z_WIDTH = 3 * CONV_WIDTH + Q_RANK + KV_RANK + 4 * MLSTM_WIDTH + LANES
GATE_LANE = QK_ROPE
ROUTER_LANE = N_GROUPS

VMEM_LIMIT = 56 * 1024 * 1024

TOKEN_TILE = 512
ATTN_TILE = 512
MLSTM_CHUNK = 256
COMBINE_TILE = 256
PROMPT_EXPERT_BLOCK = 256
SAMPLE_EXPERT_BLOCK = 16
PAGES_PER_CHUNK = 64
STEP_SEQS = 1


def _cparams(n_grid, row_dma=False):
    return pltpu.CompilerParams(dimension_semantics=("arbitrary",) * n_grid, vmem_limit_bytes=VMEM_LIMIT,
                                disable_bounds_checks=row_dma)


def _layer_norm(x, g, b):
    mu = jnp.mean(x, -1, keepdims=True)
    xc = x - mu
    var = jnp.mean(xc * xc, -1, keepdims=True)
    return xc * lax.rsqrt(var + LN_EPS) * g + b


def _rms_norm(x, g):
    return x * lax.rsqrt(jnp.mean(x * x, -1, keepdims=True) + RMS_EPS) * g


def _log_sigmoid(x):
    return jnp.minimum(x, 0.0) - jnp.log1p(jnp.exp(-jnp.abs(x)))


def _sigmoid(x):
    return 1.0 / (1.0 + jnp.exp(-x))


def _mm(a, w):
    if w.dtype == BF16:
        return jnp.dot(a.astype(BF16), w, preferred_element_type=F32)
    return jnp.dot(a.astype(F32), w, precision=lax.Precision.HIGHEST, preferred_element_type=F32)


def _rope_block(x, cos_t, sin_t, first_lane):
    lane = lax.broadcasted_iota(jnp.int32, x.shape, 1)
    in_first = (lane >= first_lane) & (lane < first_lane + ROPE_HALF)
    partner = jnp.where(in_first, pltpu.roll(x, LANES - ROPE_HALF, 1), pltpu.roll(x, ROPE_HALF, 1))
    return x * cos_t + partner * sin_t


def _in_proj_kernel(*refs, apply_ln0, seq_mode, emit_kv, tm):
    it = iter(refs)
    x_ref = next(it)
    if apply_ln0:
        g0_ref, b0_ref = next(it), next(it)
    w_in_ref, convw_ref = next(it), next(it)
    if not seq_mode:
        hist0_ref, hist1_ref = next(it), next(it)
    qg_ref, wuq_ref, kvg_ref = next(it), next(it), next(it)
    if emit_kv:
        wuk_ref, wuv_ref = next(it), next(it)
    gb_ref, cq_ref, sq_ref, ck_ref, sk_ref = next(it), next(it), next(it), next(it), next(it)
    if apply_ln0:
        xn_ref = next(it)
    yconv_ref, q_ref, ckv_ref, krope_ref = next(it), next(it), next(it), next(it)
    mqT_ref, mk_ref, mvT_ref, og_ref, gates_ref, gatesT_ref = (next(it) for _ in range(6))
    if emit_kv:
        k_ref, v_ref = next(it), next(it)
    if seq_mode:
        cstate_ref = next(it)
        ubuf = next(it)
    else:
        u_ref = next(it)

    lane = lax.broadcasted_iota(jnp.int32, (tm, LANES), 1)
    x = x_ref[...]
    if apply_ln0:
        x = _layer_norm(x, g0_ref[...], b0_ref[...])
        xn_ref[...] = x
    z = _mm(x, w_in_ref[...])
    o = 0
    cb = z[:, o:o + CONV_WIDTH]; o += CONV_WIDTH
    cc = z[:, o:o + CONV_WIDTH]; o += CONV_WIDTH
    ch = z[:, o:o + CONV_WIDTH]; o += CONV_WIDTH
    c_q = z[:, o:o + Q_RANK]; o += Q_RANK
    c_kv = z[:, o:o + KV_RANK]; o += KV_RANK
    mq = z[:, o:o + MLSTM_WIDTH]; o += MLSTM_WIDTH
    mk = z[:, o:o + MLSTM_WIDTH]; o += MLSTM_WIDTH
    mv = z[:, o:o + MLSTM_WIDTH]; o += MLSTM_WIDTH
    mo = z[:, o:o + MLSTM_WIDTH]; o += MLSTM_WIDTH
    last = z[:, o:o + LANES]

    u = cc * ch
    w = convw_ref[...]
    if seq_mode:
        s_idx = pl.program_id(1)

        @pl.when(s_idx == 0)
        def _():
            ubuf[0:8, :] = jnp.zeros((8, CONV_WIDTH), F32)

        ubuf[8:8 + tm, :] = u
        conv = ubuf[6:6 + tm, :] * w[0:1] + ubuf[7:7 + tm, :] * w[1:2] + u * w[2:3]
        ubuf[0:8, :] = ubuf[tm:tm + 8, :]

        @pl.when(s_idx == pl.num_programs(1) - 1)
        def _():
            cstate_ref[...] = ubuf[6:8, :]
    else:
        conv = hist0_ref[...] * w[0:1] + hist1_ref[...] * w[1:2] + u * w[2:3]
        u_ref[...] = u
    yconv_ref[...] = cb * conv

    cqn = _rms_norm(c_q, qg_ref[...])
    q = _mm(cqn, wuq_ref[...])
    cq, sq = cq_ref[...], sq_ref[...]
    for h in range(MLA_HEADS):
        qh = _rope_block(q[:, h * LANES:(h + 1) * LANES], cq, sq, QK_NOPE) * Q_FOLD
        q_ref[:, h * LANES:(h + 1) * LANES] = qh.astype(q_ref.dtype)

    ckv = _rms_norm(c_kv, kvg_ref[...])
    ckv_ref[...] = ckv
    kr = _rope_block(last, ck_ref[...], sk_ref[...], 0)
    krope_ref[...] = kr[:, 0:QK_ROPE]
    if emit_kv:
        ckv_b = ckv.astype(BF16)
        kfull = jnp.dot(ckv_b, wuk_ref[...], preferred_element_type=F32)
        kr_shift = pltpu.roll(kr, QK_NOPE, 1)
        for h in range(MLA_HEADS):
            k_ref[:, h * LANES:(h + 1) * LANES] = (kfull[:, h * LANES:(h + 1) * LANES] + kr_shift).astype(BF16)
        vfull = jnp.dot(ckv_b, wuv_ref[...], preferred_element_type=F32)
        ones_hi = jnp.where(lane >= V_DIM, 1.0, 0.0)
        for h in range(MLA_HEADS):
            v_ref[:, h * LANES:(h + 1) * LANES] = (vfull[:, h * LANES:(h + 1) * LANES] + ones_hi).astype(BF16)

    mqT_ref[...] = mq.T.astype(mqT_ref.dtype)
    mk_ref[...] = (mk * (MLSTM_DH ** -0.5)).astype(mk_ref.dtype)
    mvT_ref[...] = mv.T.astype(mvT_ref.dtype)
    og_ref[...] = mo
    g = last + gb_ref[...]
    is_forget = (lane >= GATE_LANE + MLSTM_HEADS) & (lane < GATE_LANE + 2 * MLSTM_HEADS)
    g = jnp.where(is_forget, _log_sigmoid(g), g)
    gates_ref[...] = g[:, GATE_LANE:GATE_LANE + 2 * MLSTM_HEADS]
    gatesT_ref[...] = g.T[GATE_LANE:GATE_LANE + 2 * MLSTM_HEADS, :]


def _in_proj(x, lw, tabs, *, apply_ln0, ln0, seq_mode, emit_kv, hist=None):
    B, S, D = x.shape
    tm = min(TOKEN_TILE, S)
    assert S % tm == 0
    ns = S // tm
    tok = lambda w: pl.BlockSpec((None, tm, w), lambda b, s: (b, s, 0))
    full2 = lambda a: pl.BlockSpec(a.shape, lambda b, s: (0, 0))
    tab = pl.BlockSpec((tm, LANES), lambda b, s: (s, 0))

    ins, specs = [x], [tok(D)]
    if apply_ln0:
        ins += [ln0[0], ln0[1]]; specs += [full2(ln0[0]), full2(ln0[1])]
    ins += [lw["w_in"], lw["conv_w"]]; specs += [full2(lw["w_in"]), full2(lw["conv_w"])]
    if not seq_mode:
        ins += [hist[0], hist[1]]; specs += [tok(CONV_WIDTH), tok(CONV_WIDTH)]
    ins += [lw["q_norm_g"], lw["w_uq"], lw["kv_norm_g"]]
    specs += [full2(lw["q_norm_g"]), full2(lw["w_uq"]), full2(lw["kv_norm_g"])]
    if emit_kv:
        ins += [lw["w_uk"], lw["w_uv"]]; specs += [full2(lw["w_uk"]), full2(lw["w_uv"])]
    ins += [lw["gate_b"]] + list(tabs)
    specs += [full2(lw["gate_b"])] + [tab] * 4

    outs, ospecs = [], []

    def add(shape, dtype, spec):
        outs.append(jax.ShapeDtypeStruct(shape, dtype)); ospecs.append(spec)

    if apply_ln0:
        add((B, S, D), F32, tok(D))
    add((B, S, CONV_WIDTH), F32, tok(CONV_WIDTH))
    add((B, S, MLA_HEADS * LANES), BF16 if emit_kv else F32, tok(MLA_HEADS * LANES))
    add((B, S, KV_RANK), F32, tok(KV_RANK))
    add((B, S, QK_ROPE), F32, tok(QK_ROPE))
    mdt = BF16 if seq_mode else F32
    tokT = pl.BlockSpec((None, MLSTM_WIDTH, tm), lambda b, s: (b, 0, s))
    add((B, MLSTM_WIDTH, S), mdt, tokT)
    add((B, S, MLSTM_WIDTH), mdt, tok(MLSTM_WIDTH))
    add((B, MLSTM_WIDTH, S), mdt, tokT)
    add((B, S, MLSTM_WIDTH), F32, tok(MLSTM_WIDTH))
    add((B, S, 2 * MLSTM_HEADS), F32, tok(2 * MLSTM_HEADS))
    add((B, 2 * MLSTM_HEADS, S), F32, pl.BlockSpec((None, 2 * MLSTM_HEADS, tm), lambda b, s: (b, 0, s)))
    if emit_kv:
        add((B, S, MLA_HEADS * LANES), BF16, tok(MLA_HEADS * LANES))
        add((B, S, MLA_HEADS * LANES), BF16, tok(MLA_HEADS * LANES))
    scratch = []
    if seq_mode:
        add((B, CONV_K - 1, CONV_WIDTH), F32, pl.BlockSpec((None, CONV_K - 1, CONV_WIDTH), lambda b, s: (b, 0, 0)))
        scratch = [pltpu.VMEM((tm + 8, CONV_WIDTH), F32)]
    else:
        add((B, S, CONV_WIDTH), F32, tok(CONV_WIDTH))
    kern = functools.partial(_in_proj_kernel, apply_ln0=apply_ln0, seq_mode=seq_mode, emit_kv=emit_kv, tm=tm)
    return pl.pallas_call(kern, grid=(B, ns), in_specs=specs, out_specs=ospecs, out_shape=outs,
                          scratch_shapes=scratch, compiler_params=_cparams(2), name="in_proj")(*ins)


def _flash_attn_kernel(q_ref, k_ref, v_ref, o_ref, *, tq):
    qi = pl.program_id(2)
    row = lax.broadcasted_iota(jnp.int32, (tq, tq), 0)
    col = lax.broadcasted_iota(jnp.int32, (tq, tq), 1)
    causal = col <= row

    def block(ki, carry, masked):
        start = pl.multiple_of(ki * tq, tq)
        new = []
        for hh in range(2):
            m, acc = carry[hh]
            hl = slice(hh * LANES, (hh + 1) * LANES)
            k = k_ref[pl.ds(start, tq), hl]
            v = v_ref[pl.ds(start, tq), hl]
            s = lax.dot_general(q_ref[:, hl], k, (((1,), (1,)), ((), ())), preferred_element_type=F32)
            if masked:
                s = jnp.where(causal, s, -jnp.inf)
            m_new = jnp.maximum(m, jnp.max(s, -1, keepdims=True))
            p = jnp.exp2(s - m_new)
            acc = jnp.exp2(m - m_new) * acc + jnp.dot(p.astype(BF16), v, preferred_element_type=F32)
            new.append((m_new, acc))
        return tuple(new)

    init = ((jnp.full((tq, 1), -jnp.inf, F32), jnp.zeros((tq, LANES), F32)),) * 2

    def two_blocks(j, carry):
        return block(2 * j + 1, block(2 * j, carry, masked=False), masked=False)

    carry = lax.fori_loop(0, qi // 2, two_blocks, init)
    carry = lax.cond(qi % 2 == 1, lambda c: block(qi - 1, c, masked=False), lambda c: c, carry)
    fin = block(qi, carry, masked=True)
    outs = [acc / pltpu.roll(acc, V_DIM, 1) for _, acc in fin]
    lane = lax.broadcasted_iota(jnp.int32, (tq, LANES), 1)
    o_ref[...] = jnp.where(lane < V_DIM, outs[0], pltpu.roll(outs[1], V_DIM, 1))


def _flash_attn(q, k, v):
    B, S, _ = q.shape
    tq = min(ATTN_TILE, S)
    assert S % tq == 0
    kern = functools.partial(_flash_attn_kernel, tq=tq)
    return pl.pallas_call(
        kern, grid=(B, MLA_HEADS // 2, S // tq),
        in_specs=[pl.BlockSpec((None, tq, 2 * LANES), lambda b, h, i: (b, i, h)),
                  pl.BlockSpec((None, S, 2 * LANES), lambda b, h, i: (b, 0, h)),
                  pl.BlockSpec((None, S, 2 * LANES), lambda b, h, i: (b, 0, h))],
        out_specs=pl.BlockSpec((None, tq, 2 * V_DIM), lambda b, h, i: (b, i, h)),
        out_shape=jax.ShapeDtypeStruct((B, S, MLA_HEADS * V_DIM), F32),
        compiler_params=_cparams(3), name="flash_attn")(q, k, v)


def _paged_attn_kernel(pt_ref, q_ref, ckvn_ref, krn_ref, wukT_ref, wuv_ref, cckv_ref, ckrT_ref, o_ref,
                       kbuf, rbuf, sems, qlat_scr, qr_scr, m_scr, l_scr, acc_scr, *, layer, npc, nchunks, page):
    b, c = pl.program_id(0), pl.program_id(1)
    g = b * nchunks + c
    total = pl.num_programs(0) * nchunks
    slot = g % 2

    def page_copies(bb, cc, sl, j):
        pg = pt_ref[bb, cc * npc + j]
        return (pltpu.make_async_copy(cckv_ref.at[layer, pg], kbuf.at[sl, j], sems.at[0, sl]),
                pltpu.make_async_copy(ckrT_ref.at[layer, pg], rbuf.at[sl, j], sems.at[1, sl]))

    def start_chunk(bb, cc, sl):
        def issue(j, _):
            for cp in page_copies(bb, cc, sl, j):
                cp.start()
            return 0
        lax.fori_loop(0, npc, issue, 0, unroll=4)

    @pl.when(g == 0)
    def _():
        start_chunk(b, c, slot)

    @pl.when(g + 1 < total)
    def _():
        nxt = g + 1
        start_chunk(nxt // nchunks, nxt % nchunks, 1 - slot)

    @pl.when(c == 0)
    def _():
        qrow = q_ref[0]
        for h in range(MLA_HEADS):
            qn = jnp.broadcast_to(qrow[:, h * LANES:h * LANES + QK_NOPE], (8, QK_NOPE)).astype(BF16)
            qlat_scr[h:h + 1, :] = jnp.dot(qn, wukT_ref[h], preferred_element_type=F32)[0:1]
            qr_scr[h:h + 1, :] = qrow[:, h * LANES + QK_NOPE:h * LANES + QK_NOPE + QK_ROPE]
        m_scr[...] = jnp.full(m_scr.shape, -jnp.inf, F32)
        l_scr[...] = jnp.zeros(l_scr.shape, F32)
        acc_scr[...] = jnp.zeros(acc_scr.shape, F32)

    pltpu.make_async_copy(cckv_ref.at[layer, pl.ds(0, npc)], kbuf.at[slot], sems.at[0, slot]).wait()
    pltpu.make_async_copy(ckrT_ref.at[layer, pl.ds(0, npc)], rbuf.at[slot], sems.at[1, slot]).wait()

    kb = kbuf[slot].reshape(npc * page, KV_RANK).astype(BF16)
    qr = qr_scr[...].astype(BF16)
    s_rope = jnp.concatenate([jnp.dot(qr, rbuf[slot, j].astype(BF16), preferred_element_type=F32)
                              for j in range(npc)], axis=-1)
    s = lax.dot_general(qlat_scr[...].astype(BF16), kb, (((1,), (1,)), ((), ())),
                        preferred_element_type=F32) + s_rope
    m = m_scr[...]
    m_new = jnp.maximum(m, jnp.max(s, -1, keepdims=True))
    alpha = jnp.exp2(m - m_new)
    p = jnp.exp2(s - m_new)
    l_scr[...] = alpha * l_scr[...] + jnp.sum(p, -1, keepdims=True)
    acc_scr[...] = alpha * acc_scr[...] + jnp.dot(p.astype(BF16), kb, preferred_element_type=F32)
    m_scr[...] = m_new

    @pl.when(c == nchunks - 1)
    def _():
        ckv_new, kr_new = ckvn_ref[0], krn_ref[0]
        s_new = (jnp.sum(qlat_scr[...] * ckv_new, -1, keepdims=True)
                 + jnp.sum(qr_scr[...] * kr_new, -1, keepdims=True))
        m_old = m_scr[...]
        m_fin = jnp.maximum(m_old, s_new)
        a = jnp.exp2(m_old - m_fin)
        p_new = jnp.exp2(s_new - m_fin)
        l_fin = a * l_scr[...] + p_new
        o_lat = (a * acc_scr[...] + p_new * ckv_new) / l_fin
        for h in range(MLA_HEADS):
            oh = jnp.broadcast_to(o_lat[h:h + 1, :], (8, KV_RANK)).astype(BF16)
            o_ref[0, :, h * V_DIM:(h + 1) * V_DIM] = jnp.dot(oh, wuv_ref[h], preferred_element_type=F32)[0:1]


def _paged_attn(q, ckv_new, kr_new, wukT, wuv_h, cache_ckv, cache_krope_t, page_table, layer):
    DB = q.shape[0]
    n_pages = page_table.shape[1]
    page = cache_ckv.shape[2]
    npc = min(PAGES_PER_CHUNK, n_pages)
    assert n_pages % npc == 0
    nchunks = n_pages // npc
    kern = functools.partial(_paged_attn_kernel, layer=layer, npc=npc, nchunks=nchunks, page=page)
    row = lambda w: pl.BlockSpec((1, 1, w), lambda b, c, pt: (b, 0, 0))
    full3 = lambda a: pl.BlockSpec(a.shape, lambda b, c, pt: (0, 0, 0))
    gs = pltpu.PrefetchScalarGridSpec(
        num_scalar_prefetch=1, grid=(DB, nchunks),
        in_specs=[row(MLA_HEADS * LANES), row(KV_RANK), row(QK_ROPE), full3(wukT), full3(wuv_h),
                  pl.BlockSpec(memory_space=pl.ANY), pl.BlockSpec(memory_space=pl.ANY)],
        out_specs=row(MLA_HEADS * V_DIM),
        scratch_shapes=[pltpu.VMEM((2, npc, page, KV_RANK), F32), pltpu.VMEM((2, npc, QK_ROPE, page), F32),
                        pltpu.SemaphoreType.DMA((2, 2)),
                        pltpu.VMEM((MLA_HEADS, KV_RANK), F32), pltpu.VMEM((MLA_HEADS, QK_ROPE), F32),
                        pltpu.VMEM((MLA_HEADS, 1), F32), pltpu.VMEM((MLA_HEADS, 1), F32),
                        pltpu.VMEM((MLA_HEADS, KV_RANK), F32)])
    return pl.pallas_call(kern, grid_spec=gs, out_shape=jax.ShapeDtypeStruct((DB, 1, MLA_HEADS * V_DIM), F32),
                          compiler_params=_cparams(2, row_dma=True), name="paged_attn")(
        page_table, q, ckv_new, kr_new, wukT, wuv_h, cache_ckv, cache_krope_t)


def _mlstm_seq_kernel(qT_ref, k_ref, vT_ref, g_ref, gT_ref, h_ref, c_out, n_out, m_out, c_scr, n_scr, m_scr, hT_scr, *, L):
    c = pl.program_id(1)

    @pl.when(c == 0)
    def _():
        c_scr[...] = jnp.zeros(c_scr.shape, F32)
        n_scr[...] = jnp.zeros(n_scr.shape, F32)
        m_scr[...] = jnp.zeros(m_scr.shape, F32)

    g = g_ref[...]
    gT = gT_ref[...]
    src = lax.broadcasted_iota(jnp.int32, (L, L), 0)
    tgt = lax.broadcasted_iota(jnp.int32, (L, L), 1)
    causal = src <= tgt
    b_cols = jnp.dot((tgt <= src).astype(F32), g, precision=lax.Precision.HIGHEST, preferred_element_type=F32)
    b_rows = jnp.dot(gT, causal.astype(F32), precision=lax.Precision.HIGHEST, preferred_element_type=F32)
    for h in range(MLSTM_HEADS):
        sl = slice(h * MLSTM_DH, (h + 1) * MLSTM_DH)
        qT, kh, vT = qT_ref[sl, :], k_ref[:, sl], vT_ref[sl, :]
        a_col = g[:, h:h + 1] - b_cols[:, MLSTM_HEADS + h:MLSTM_HEADS + h + 1]
        b_row = b_rows[MLSTM_HEADS + h:MLSTM_HEADS + h + 1, :]
        i_row = gT[h:h + 1, :]
        m_prev = m_scr[h:h + 1, 0:1]
        log_d = jnp.where(causal, a_col + b_row, -jnp.inf)
        m_inter = b_row + m_prev
        m_t = jnp.maximum(m_inter, jnp.max(log_d, 0, keepdims=True))
        w_inter = jnp.exp(m_inter - m_t)
        sT = jnp.dot(kh, qT, preferred_element_type=F32) * jnp.exp(log_d - m_t)
        C = c_scr[h]
        n8 = jnp.broadcast_to(n_scr[h:h + 1, :], (8, MLSTM_DH)).astype(BF16)
        numT = (jnp.dot(vT, sT.astype(BF16), preferred_element_type=F32)
                + w_inter * jnp.dot(C.astype(BF16), qT, preferred_element_type=F32))
        den = jnp.sum(sT, 0, keepdims=True) + w_inter * jnp.dot(n8, qT, preferred_element_type=F32)[0:1]
        hT_scr[sl, :] = numT / jnp.maximum(jnp.abs(den), jnp.exp(-m_t))
        m_new = m_t[:, L - 1:L]
        b_last = b_row[:, L - 1:L]
        w_old = jnp.exp(b_last + m_prev - m_new)
        w_s = jnp.exp(b_last - b_row + i_row - m_new)
        c_scr[h] = w_old * C + jnp.dot((vT.astype(F32) * w_s).astype(BF16), kh, preferred_element_type=F32)
        ws8 = jnp.broadcast_to(w_s, (8, L)).astype(BF16)
        n_scr[h:h + 1, :] = w_old * n_scr[h:h + 1, :] + jnp.dot(ws8, kh, preferred_element_type=F32)[0:1]
        m_scr[h:h + 1, :] = jnp.broadcast_to(m_new, (1, LANES))
    h_ref[...] = hT_scr[...].T

    @pl.when(c == pl.num_programs(1) - 1)
    def _():
        c_out[...] = c_scr[...]
        n_out[...] = n_scr[...]
        m_out[...] = m_scr[...]


def _eye(n):
    return (lax.broadcasted_iota(jnp.int32, (n, n), 0) == lax.broadcasted_iota(jnp.int32, (n, n), 1)).astype(F32)


def _col_to_row(col):
    return jnp.sum(_eye(col.shape[0]) * col, 0, keepdims=True)


def _row_to_col(row):
    return jnp.sum(_eye(row.shape[1]) * row, 1, keepdims=True)


def _mlstm_seq(mqT, mk, mvT, gates, gatesT):
    B, S, _ = mk.shape
    L = min(MLSTM_CHUNK, S)
    assert S % L == 0
    kern = functools.partial(_mlstm_seq_kernel, L=L)
    tok = lambda w: pl.BlockSpec((None, L, w), lambda b, c: (b, c, 0))
    tokT = lambda w: pl.BlockSpec((None, w, L), lambda b, c: (b, 0, c))
    st = lambda *dims: pl.BlockSpec((None,) + dims, lambda b, c: (b,) + (0,) * len(dims))
    return pl.pallas_call(
        kern, grid=(B, S // L),
        in_specs=[tokT(MLSTM_WIDTH), tok(MLSTM_WIDTH), tokT(MLSTM_WIDTH), tok(2 * MLSTM_HEADS), tokT(2 * MLSTM_HEADS)],
        out_specs=[tok(MLSTM_WIDTH), st(MLSTM_HEADS, MLSTM_DH, MLSTM_DH), st(MLSTM_HEADS, MLSTM_DH), st(8, LANES)],
        out_shape=[jax.ShapeDtypeStruct((B, S, MLSTM_WIDTH), F32),
                   jax.ShapeDtypeStruct((B, MLSTM_HEADS, MLSTM_DH, MLSTM_DH), F32),
                   jax.ShapeDtypeStruct((B, MLSTM_HEADS, MLSTM_DH), F32),
                   jax.ShapeDtypeStruct((B, 8, LANES), F32)],
        scratch_shapes=[pltpu.VMEM((MLSTM_HEADS, MLSTM_DH, MLSTM_DH), F32), pltpu.VMEM((MLSTM_HEADS, MLSTM_DH), F32),
                        pltpu.VMEM((8, LANES), F32), pltpu.VMEM((MLSTM_WIDTH, L), F32)],
        compiler_params=_cparams(2), name="mlstm_seq")(mqT, mk, mvT, gates, gatesT)


def _mlstm_step_kernel(q_ref, k_ref, v_ref, g_ref, c_ref, n_ref, m_ref, h_ref, c_out, n_out, m_out):
    for i in range(q_ref.shape[0]):
        g = g_ref[i]
        m_all = m_ref[i]
        for h in range(MLSTM_HEADS):
            sl = slice(h * MLSTM_DH, (h + 1) * MLSTM_DH)
            q, k, v = q_ref[i][:, sl], k_ref[i][:, sl], v_ref[i][:, sl]
            C, n = c_ref[i, h], n_ref[i, h:h + 1, :]
            ig, lf = g[:, h:h + 1], g[:, MLSTM_HEADS + h:MLSTM_HEADS + h + 1]
            m_prev = m_all[:, h:h + 1]
            m_inter = lf + m_prev
            m_t = jnp.maximum(m_inter, ig)
            d = jnp.exp(ig - m_t)
            w_inter = jnp.exp(m_inter - m_t)
            s = jnp.sum(q * k, -1, keepdims=True) * d
            cq = jnp.sum(C * q, -1, keepdims=True)
            v_col = _row_to_col(v)
            num = s * v_col + w_inter * cq
            den = s + w_inter * jnp.sum(n * q, -1, keepdims=True)
            h_col = num / jnp.maximum(jnp.abs(den), jnp.exp(-m_t))
            h_ref[i, :, sl] = _col_to_row(h_col)
            c_out[i, h] = w_inter * C + d * (v_col * k)
            n_out[i, h:h + 1, :] = w_inter * n + d * k
            m_out[i, :, h:h + 1] = m_t


def _mlstm_step(mq, mk, mv, gates, C0, n0, m0):
    DB = mq.shape[0]
    sb = STEP_SEQS if DB % STEP_SEQS == 0 else 1
    r3 = lambda a: a.reshape(DB, 1, a.shape[-1])
    row = lambda w: pl.BlockSpec((sb, 1, w), lambda b: (b, 0, 0))
    cspec = pl.BlockSpec((sb, MLSTM_HEADS, MLSTM_DH, MLSTM_DH), lambda b: (b, 0, 0, 0))
    nspec = pl.BlockSpec((sb, MLSTM_HEADS, MLSTM_DH), lambda b: (b, 0, 0))
    return pl.pallas_call(
        _mlstm_step_kernel, grid=(DB // sb,),
        in_specs=[row(MLSTM_WIDTH)] * 3 + [row(2 * MLSTM_HEADS), cspec, nspec, row(MLSTM_HEADS)],
        out_specs=[row(MLSTM_WIDTH), cspec, nspec, row(MLSTM_HEADS)],
        out_shape=[jax.ShapeDtypeStruct((DB, 1, MLSTM_WIDTH), F32), jax.ShapeDtypeStruct(C0.shape, F32),
                   jax.ShapeDtypeStruct(n0.shape, F32), jax.ShapeDtypeStruct((DB, 1, MLSTM_HEADS), F32)],
        compiler_params=_cparams(1), name="mlstm_step")(r3(mq), r3(mk), r3(mv), r3(gates), C0, n0, r3(m0))


def _mix_out_kernel(yc_ref, att_ref, hm_ref, og_ref, x_ref, mg_ref, wout_ref, g1_ref, b1_ref, wr_ref, br_ref,
                    h1_ref, route_ref, cnt_ref, ymix_scr, carry_scr, *, tm, alpha):
    @pl.when(pl.program_id(0) == 0)
    def _():
        carry_scr[...] = jnp.zeros(carry_scr.shape, F32)

    lane = lax.broadcasted_iota(jnp.int32, (tm, LANES), 1)
    low = lane < HEAD_DIM

    def norm_store(y, j):
        y2 = y * y
        s_lo = jnp.sum(jnp.where(low, y2, 0.0), -1, keepdims=True)
        s_hi = jnp.sum(jnp.where(low, 0.0, y2), -1, keepdims=True)
        inv = jnp.where(low, lax.rsqrt(s_lo / HEAD_DIM + RMS_EPS), lax.rsqrt(s_hi / HEAD_DIM + RMS_EPS))
        ymix_scr[:, j * LANES:(j + 1) * LANES] = (y * inv * mg_ref[:, j * LANES:(j + 1) * LANES]).astype(ymix_scr.dtype)

    j = 0
    for src, width in ((yc_ref, CONV_WIDTH), (att_ref, MLA_HEADS * V_DIM)):
        for t in range(width // LANES):
            norm_store(src[:, t * LANES:(t + 1) * LANES], j)
            j += 1
    for t in range(MLSTM_WIDTH // LANES):
        sl = slice(t * LANES, (t + 1) * LANES)
        norm_store(_sigmoid(og_ref[:, sl]) * hm_ref[:, sl], j)
        j += 1

    proj = _mm(ymix_scr[...], wout_ref[...])
    h1 = _layer_norm(alpha * x_ref[...] + proj, g1_ref[...], b1_ref[...])
    h1_ref[...] = h1

    h_hi = h1.astype(BF16)
    h_lo = (h1 - h_hi.astype(F32)).astype(BF16)
    part = jnp.dot(h_hi, wr_ref[...], preferred_element_type=F32)
    logits = (part[:, 0:LANES] + part[:, LANES:2 * LANES]
              + jnp.dot(h_lo, wr_ref[:, 0:LANES], preferred_element_type=F32) + br_ref[...])
    lanef = lane.astype(F32)
    big = float(LANES)
    neg = -jnp.inf
    is_g = lane < N_GROUPS
    gl = jnp.where(is_g, logits, neg)
    gmax = jnp.max(gl, -1, keepdims=True)
    g_sel = jnp.min(jnp.where(gl == gmax, lanef, big), -1, keepdims=True)
    g_w = 1.0 / jnp.sum(jnp.where(is_g, jnp.exp(logits - gmax), 0.0), -1, keepdims=True)
    lo = ROUTER_LANE + EXPERTS_PER_GROUP * g_sel
    in_grp = (lanef >= lo) & (lanef < lo + EXPERTS_PER_GROUP)
    el = jnp.where(in_grp, logits, neg)
    e1 = jnp.max(el, -1, keepdims=True)
    i1 = jnp.min(jnp.where(el == e1, lanef, big), -1, keepdims=True)
    el2 = jnp.where(lanef == i1, neg, el)
    e2 = jnp.max(el2, -1, keepdims=True)
    i2 = jnp.min(jnp.where(el2 == e2, lanef, big), -1, keepdims=True)
    zsum = jnp.sum(jnp.where(in_grp, jnp.exp(logits - e1), 0.0), -1, keepdims=True)
    p1 = 1.0 / zsum
    p2 = jnp.exp(e2 - e1) / zsum
    psum = p1 + p2
    w1 = p1 / psum * g_w
    w2 = p2 / psum * g_w

    oh1 = (lanef == i1).astype(F32)
    oh2 = (lanef == i2).astype(F32)
    both = oh1 + oh2
    r = lax.broadcasted_iota(jnp.int32, (tm, tm), 0)
    c = lax.broadcasted_iota(jnp.int32, (tm, tm), 1)
    before = jnp.dot((c < r).astype(BF16), both.astype(BF16), preferred_element_type=F32) + carry_scr[0:1, :]
    r1 = jnp.sum(oh1 * before, -1, keepdims=True)
    r2 = jnp.sum(oh2 * before, -1, keepdims=True)
    carry = carry_scr[0:1, :] + jnp.sum(both, 0, keepdims=True)
    carry_scr[...] = jnp.broadcast_to(carry, carry_scr.shape)
    cnt_ref[...] = jnp.broadcast_to(carry, cnt_ref.shape)

    vals = (i1 - ROUTER_LANE, i2 - ROUTER_LANE, r1, r2, w1, w2)
    packed = jnp.zeros((tm, LANES), F32)
    for idx, val in enumerate(vals):
        packed = jnp.where(lane == idx, val, packed)
    route_ref[...] = packed[:, 0:8]


def _mix_out(yconv, att, hm, og, x, lw, alpha):
    T, D = x.shape
    tm = min(TOKEN_TILE, T)
    assert T % tm == 0
    tok = lambda w: pl.BlockSpec((tm, w), lambda i: (i, 0))
    full = lambda a: pl.BlockSpec(a.shape, lambda i: (0, 0))
    ws = [lw["mix_norm_g"], lw["w_out"], lw["ln1_g"], lw["ln1_b"], lw["w_router"], lw["b_router"]]
    kern = functools.partial(_mix_out_kernel, tm=tm, alpha=alpha)
    return pl.pallas_call(
        kern, grid=(T // tm,),
        in_specs=[tok(CONV_WIDTH), tok(MLA_HEADS * V_DIM), tok(MLSTM_WIDTH), tok(MLSTM_WIDTH), tok(D)] + [full(a) for a in ws],
        out_specs=[tok(D), tok(8), pl.BlockSpec((8, LANES), lambda i: (0, 0))],
        out_shape=[jax.ShapeDtypeStruct((T, D), F32), jax.ShapeDtypeStruct((T, 8), F32),
                   jax.ShapeDtypeStruct((8, LANES), F32)],
        scratch_shapes=[pltpu.VMEM((tm, D), lw["w_out"].dtype), pltpu.VMEM((8, LANES), F32)],
        compiler_params=_cparams(1), name="mix_out")(yconv, att, hm, og, x, *ws)


def _ple_dispatch_kernel(dest_ref, h1_ref, p_ref, wpe_ref, wpg_ref, xs_in, ple_ref, xs_out, sem, *, tm):
    del xs_in

    def issue(r, _):
        for k in range(TOP_K):
            pltpu.make_async_copy(h1_ref.at[pl.ds(r, 1)], xs_out.at[pl.ds(dest_ref[TOP_K * r + k], 1)], sem).start()
        return 0
    lax.fori_loop(0, tm, issue, 0, unroll=8)

    ple_ref[...] = _mm(p_ref[...], wpe_ref[...]) * _sigmoid(_mm(h1_ref[...], wpg_ref[...]))

    for k in range(TOP_K):
        pltpu.make_async_copy(h1_ref, xs_out.at[pl.ds(0, tm)], sem).wait()


def _ple_dispatch(dest_flat, h1, p, lw, n_rows):
    T, D = h1.shape
    tm = min(TOKEN_TILE, T)
    tok = lambda w: pl.BlockSpec((tm, w), lambda i: (i, 0))
    full = lambda a: pl.BlockSpec(a.shape, lambda i: (0, 0))
    xs0 = jnp.zeros((n_rows, D), F32)
    kern = functools.partial(_ple_dispatch_kernel, tm=tm)
    ple, xs = pl.pallas_call(
        kern, grid=(T // tm,),
        in_specs=[pl.BlockSpec((TOP_K * tm,), lambda i: (i,), memory_space=pltpu.SMEM),
                  tok(D), tok(p.shape[1]), full(lw["w_pe"]), full(lw["w_pg"]), pl.BlockSpec(memory_space=pl.ANY)],
        out_specs=[tok(D), pl.BlockSpec(memory_space=pl.ANY)],
        out_shape=[jax.ShapeDtypeStruct((T, D), F32), jax.ShapeDtypeStruct((n_rows, D), F32)],
        scratch_shapes=[pltpu.SemaphoreType.DMA(())],
        input_output_aliases={5: 1},
        compiler_params=_cparams(1, row_dma=True), name="ple_dispatch")(dest_flat, h1, p, lw["w_pe"], lw["w_pg"], xs0)
    return ple, xs


def _experts_kernel(be_ref, nu_ref, xs_ref, wg_ref, wu_ref, wd_ref, ys_ref):
    del be_ref
    i = pl.program_id(0)

    @pl.when(i < nu_ref[0])
    def _():
        x = xs_ref[...]
        gt, up = _mm(x, wg_ref[0]), _mm(x, wu_ref[0])
        ys_ref[...] = _mm(gt * _sigmoid(gt) * up, wd_ref[0])

    @pl.when(i >= nu_ref[0])
    def _():
        ys_ref[...] = jnp.zeros(ys_ref.shape, F32)


def _experts(blk_expert, n_used, xs, lw, blk):
    n_rows, D = xs.shape
    layer = lw["layer"]
    wspec = lambda a: pl.BlockSpec((None, 1) + a.shape[2:], lambda i, be, nu: (layer, be[i], 0, 0))
    gs = pltpu.PrefetchScalarGridSpec(
        num_scalar_prefetch=2, grid=(n_rows // blk,),
        in_specs=[pl.BlockSpec((blk, D), lambda i, be, nu: (i, 0)),
                  wspec(lw["e_gate"]), wspec(lw["e_up"]), wspec(lw["e_down"])],
        out_specs=pl.BlockSpec((blk, D), lambda i, be, nu: (i, 0)))
    return pl.pallas_call(_experts_kernel, grid_spec=gs, out_shape=jax.ShapeDtypeStruct((n_rows, D), F32),
                          compiler_params=_cparams(1), name="experts")(
        blk_expert, n_used, xs, lw["e_gate"], lw["e_up"], lw["e_down"])


def _combine_kernel(dest_ref, h1_ref, ple_ref, route_ref, g2_ref, b2_ref, ys_hbm, out_ref, gbuf, sem, *, tm, alpha):
    def issue(r, _):
        for k in range(TOP_K):
            pltpu.make_async_copy(ys_hbm.at[pl.ds(dest_ref[TOP_K * r + k], 1)], gbuf.at[k, pl.ds(r, 1)], sem).start()
        return 0
    lax.fori_loop(0, tm, issue, 0, unroll=8)

    for k in range(TOP_K):
        pltpu.make_async_copy(ys_hbm.at[pl.ds(0, tm)], gbuf.at[k], sem).wait()

    route = route_ref[...]
    moe = gbuf[0] * route[:, 4:5] + gbuf[1] * route[:, 5:6]
    out_ref[...] = _layer_norm(alpha * h1_ref[...] + moe + ple_ref[...], g2_ref[...], b2_ref[...])


def _combine(dest_flat, h1, ple, route, ys, lw, alpha):
    T, D = h1.shape
    tm = min(COMBINE_TILE, T)
    tok = lambda w: pl.BlockSpec((tm, w), lambda i: (i, 0))
    full = lambda a: pl.BlockSpec(a.shape, lambda i: (0, 0))
    kern = functools.partial(_combine_kernel, tm=tm, alpha=alpha)
    return pl.pallas_call(
        kern, grid=(T // tm,),
        in_specs=[pl.BlockSpec((TOP_K * tm,), lambda i: (i,), memory_space=pltpu.SMEM),
                  tok(D), tok(D), tok(8), full(lw["ln2_g"]), full(lw["ln2_b"]), pl.BlockSpec(memory_space=pl.ANY)],
        out_specs=tok(D), out_shape=jax.ShapeDtypeStruct((T, D), F32),
        scratch_shapes=[pltpu.VMEM((TOP_K, tm, D), F32), pltpu.SemaphoreType.DMA(())],
        compiler_params=_cparams(1, row_dma=True), name="combine")(dest_flat, h1, ple, route, lw["ln2_g"], lw["ln2_b"], ys)


def _channel_mixer(yconv, att, hm, og, x, p, lw, alpha, blk):
    T, D = x.shape
    h1, route, cnt = _mix_out(yconv, att, hm, og, x, lw, alpha)
    counts = cnt[0, ROUTER_LANE:ROUTER_LANE + N_EXPERTS].astype(jnp.int32)
    padded = (counts + blk - 1) // blk * blk
    pad_end = jnp.cumsum(padded)
    pad_start = pad_end - padded
    n_blocks = -(-T * TOP_K // blk) + N_EXPERTS
    eid = route[:, 0:TOP_K].astype(jnp.int32)
    rank = route[:, TOP_K:2 * TOP_K].astype(jnp.int32)
    dest_flat = (pad_start[eid] + rank).reshape(T * TOP_K)
    blk_first_row = jnp.arange(n_blocks, dtype=jnp.int32) * blk
    blk_expert = jnp.minimum(jnp.sum((pad_end[None, :] <= blk_first_row[:, None]).astype(jnp.int32), -1), N_EXPERTS - 1)
    n_used = (pad_end[-1:] // blk).astype(jnp.int32)
    ple, xs = _ple_dispatch(dest_flat, h1, p, lw, n_blocks * blk)
    ys = _experts(blk_expert, n_used, xs, lw, blk)
    return _combine(dest_flat, h1, ple, route, ys, lw, alpha)


def _rope_tables(pos, n_rows):
    inv = ROPE_THETA ** (-jnp.arange(ROPE_HALF, dtype=F32) / ROPE_HALF)
    ang = pos.astype(F32)[:, None] * inv
    cos, sin = jnp.cos(ang), jnp.sin(ang)
    if cos.shape[0] != n_rows:
        cos, sin = jnp.broadcast_to(cos, (n_rows, ROPE_HALF)), jnp.broadcast_to(sin, (n_rows, ROPE_HALF))
    z = lambda w: jnp.zeros((n_rows, w), F32)
    one = jnp.ones((n_rows, QK_NOPE), F32)
    pad_q = LANES - QK_NOPE - QK_ROPE
    cq = jnp.concatenate([one, cos, cos, z(pad_q)], -1)
    sq = jnp.concatenate([z(QK_NOPE), -sin, sin, z(pad_q)], -1)
    ck = jnp.concatenate([cos, cos, z(LANES - QK_ROPE)], -1)
    sk = jnp.concatenate([-sin, sin, z(LANES - QK_ROPE)], -1)
    return cq, sq, ck, sk


def _prep_layer(i, w_in, conv_w, q_norm_g, w_uq, kv_norm_g, w_ukv, mlstm_gate_b, mix_norm_g, w_out, ln1_g, ln1_b,
                w_group, b_group, w_expert, b_expert, e_gate, e_up, e_down, w_pe, w_pg, ln2_g, ln2_b, e_bf16):
    D = w_in.shape[1]
    cuts, o = [], 0
    for wdt in [CONV_WIDTH] * 3 + [Q_RANK, KV_RANK, QK_ROPE] + [MLSTM_WIDTH] * 4 + [MLSTM_HEADS] * 2:
        cuts.append((o, o + wdt)); o += wdt
    wi = w_in[i]
    col = lambda j: wi[:, cuts[j][0]:cuts[j][1]]
    tail_pad = jnp.zeros((D, LANES - QK_ROPE - 2 * MLSTM_HEADS), F32)
    w_in_p = jnp.concatenate([col(0), col(1), col(2), col(3), col(4), col(6), col(7), col(8), col(9),
                              col(5), col(10), col(11), tail_pad], -1)
    wq = w_uq[i].reshape(Q_RANK, MLA_HEADS, QK_NOPE + QK_ROPE)
    w_uq_p = jnp.pad(wq, ((0, 0), (0, 0), (0, LANES - QK_NOPE - QK_ROPE))).reshape(Q_RANK, MLA_HEADS * LANES)
    wkv = w_ukv[i].reshape(KV_RANK, MLA_HEADS, QK_NOPE + V_DIM)
    w_uk, w_uv = wkv[..., :QK_NOPE], wkv[..., QK_NOPE:]
    w_uk_p = jnp.pad(w_uk, ((0, 0), (0, 0), (0, LANES - QK_NOPE))).reshape(KV_RANK, MLA_HEADS * LANES).astype(BF16)
    gate_b = jnp.concatenate([jnp.zeros((QK_ROPE,), F32), mlstm_gate_b[i],
                              jnp.zeros((LANES - QK_ROPE - 2 * MLSTM_HEADS,), F32)])[None, :]
    w_router = jnp.concatenate([w_group[i], w_expert[i], jnp.zeros((D, LANES - N_GROUPS - N_EXPERTS), F32)], -1)
    w_router_hi = w_router.astype(BF16)
    w_router = jnp.concatenate([w_router_hi, (w_router - w_router_hi.astype(F32)).astype(BF16)], -1)
    b_router = jnp.concatenate([b_group[i], b_expert[i], jnp.zeros((LANES - N_GROUPS - N_EXPERTS,), F32)])[None, :]
    e_gate_b, e_up_b, e_down_b = e_bf16
    lw = dict(
        layer=i, w_in=w_in_p.astype(BF16), conv_w=conv_w[i], q_norm_g=q_norm_g[i][None, :], w_uq=w_uq_p.astype(BF16),
        kv_norm_g=kv_norm_g[i][None, :], w_uk=w_uk_p,
        w_uv=jnp.pad(w_uv, ((0, 0), (0, 0), (0, LANES - V_DIM))).reshape(KV_RANK, MLA_HEADS * LANES).astype(BF16),
        w_ukT_h=jnp.transpose(w_uk, (1, 2, 0)).astype(BF16),
        w_uv_h=jnp.transpose(w_uv, (1, 0, 2)).astype(BF16),
        gate_b=gate_b, mix_norm_g=mix_norm_g[i][None, :], w_out=w_out[i].astype(BF16),
        ln1_g=ln1_g[i][None, :], ln1_b=ln1_b[i][None, :], w_router=w_router, b_router=b_router,
        e_gate=e_gate_b, e_up=e_up_b, e_down=e_down_b,
        w_pe=w_pe[i].astype(BF16), w_pg=w_pg[i].astype(BF16), ln2_g=ln2_g[i][None, :], ln2_b=ln2_b[i][None, :])
    lw_s = dict(lw, w_in=w_in_p, w_uq=w_uq_p, w_out=w_out[i], e_gate=e_gate, e_up=e_up, e_down=e_down,
                w_pe=w_pe[i], w_pg=w_pg[i])
    return lw, lw_s


def kernel(x_prompt, x_sample, cache_ckv, cache_krope, state_conv, state_mlstm_C, state_mlstm_n, state_mlstm_m,
           page_table, p_prompt, p_sample, ln0_g, ln0_b, w_in, conv_w, q_norm_g, w_uq, kv_norm_g, w_ukv,
           mlstm_gate_b, mix_norm_g, w_out, ln1_g, ln1_b, w_group, b_group, w_expert, b_expert,
           e_gate, e_up, e_down, w_pe, w_pg, ln2_g, ln2_b):
    B, S, D = x_prompt.shape
    DB, DS, _ = x_sample.shape
    assert DS == 1
    depth = w_in.shape[0]
    alpha = (2 * depth) ** 0.25
    past_len = page_table.shape[1] * cache_ckv.shape[2]
    ln0 = (ln0_g[None, :], ln0_b[None, :])
    tabs_p = _rope_tables(jnp.arange(S), S)
    tabs_s = _rope_tables(jnp.full((1,), past_len), DB)
    cache_krope_t = jnp.swapaxes(cache_krope, 2, 3)
    e_bf16 = (e_gate.astype(BF16), e_up.astype(BF16), e_down.astype(BF16))

    hp = x_prompt
    hs = x_sample.reshape(1, DB, D)
    st_p, st_s = [], []
    for i in range(depth):
        lw, lw_s = _prep_layer(i, w_in, conv_w, q_norm_g, w_uq, kv_norm_g, w_ukv, mlstm_gate_b, mix_norm_g, w_out,
                               ln1_g, ln1_b, w_group, b_group, w_expert, b_expert, e_gate, e_up, e_down, w_pe, w_pg,
                               ln2_g, ln2_b, e_bf16)
        first = i == 0
        outs = _in_proj(hp, lw, tabs_p, apply_ln0=first, ln0=ln0, seq_mode=True, emit_kv=True)
        if first:
            hp, outs = outs[0], outs[1:]
        yconv, q, ckv, krope, mqT, mk, mvT, og, gates, gatesT, k, v, conv_state = outs
        att = _flash_attn(q, k, v)
        hm, C1, n1, m1 = _mlstm_seq(mqT, mk, mvT, gates, gatesT)
        T = B * S
        flat = lambda a: a.reshape(T, a.shape[-1])
        hp = _channel_mixer(flat(yconv), flat(att), flat(hm), flat(og), flat(hp), flat(p_prompt[i]), lw, alpha,
                            PROMPT_EXPERT_BLOCK).reshape(B, S, D)
        st_p.append((ckv, krope, conv_state, C1, n1, m1[:, :MLSTM_HEADS, 0]))
        hist = (state_conv[i][None, :, 0, :], state_conv[i][None, :, 1, :])
        outs = _in_proj(hs, lw_s, tabs_s, apply_ln0=first, ln0=ln0, seq_mode=False, emit_kv=False, hist=hist)
        if first:
            hs, outs = outs[0], outs[1:]
        yconv, q, ckv, krope, mqT, mk, mvT, og, gates, gatesT, u = outs
        r3 = lambda a: a.reshape(DB, 1, a.shape[-1])
        att = _paged_attn(r3(q), r3(ckv), r3(krope), lw["w_ukT_h"], lw["w_uv_h"], cache_ckv, cache_krope_t,
                          page_table, i)
        hm, C1, n1, m1 = _mlstm_step(jnp.swapaxes(mqT[0], 0, 1), mk[0], jnp.swapaxes(mvT[0], 0, 1), gates[0],
                                     state_mlstm_C[i], state_mlstm_n[i], state_mlstm_m[i])
        hs = _channel_mixer(yconv[0], att.reshape(DB, -1), hm.reshape(DB, -1), og[0], hs[0], p_sample[i].reshape(DB, -1),
                            lw_s, alpha, SAMPLE_EXPERT_BLOCK).reshape(1, DB, D)
        conv_new = jnp.stack([state_conv[i][:, 1, :], u[0]], axis=1)
        st_s.append((ckv.reshape(DB, 1, -1), krope.reshape(DB, 1, -1), conv_new, C1, n1, m1.reshape(DB, MLSTM_HEADS)))
    ckv_p, krope_p, conv_p, C_p, n_p, m_p = [jnp.stack(a) for a in zip(*st_p)]
    ckv_s, krope_s, conv_s, C_s, n_s, m_s = [jnp.stack(a) for a in zip(*st_s)]
    return (hp, hs.reshape(DB, DS, D), ckv_p, krope_p, conv_p, C_p, n_p, m_p,
            ckv_s, krope_s, conv_s, C_s, n_s, m_s)
```

```python
import functools

import jax
import jax.numpy as jnp
from jax import lax
from jax.experimental import pallas as pl
from jax.experimental.pallas import tpu as pltpu

F32 = jnp.float32
BF16 = jnp.bfloat16

LANES = 128
HEAD_DIM = 64
CONV_WIDTH = 256
CONV_K = 3
MLA_HEADS = 8
Q_RANK = 256
KV_RANK = 128
QK_NOPE = 64
QK_ROPE = 32
V_DIM = 64
ROPE_HALF = QK_ROPE // 2
ROPE_THETA = 10000.0
MLSTM_HEADS = 4
MLSTM_DH = 64
MLSTM_WIDTH = MLSTM_HEADS * MLSTM_DH
N_GROUPS = 4
EXPERTS_PER_GROUP = 8
N_EXPERTS = N_GROUPS * EXPERTS_PER_GROUP
TOP_K = 2
D_EXPERT = 256
LN_EPS = 1e-5
RMS_EPS = 1e-6
QK_SCALE = (QK_NOPE + QK_ROPE) ** -0.5
LOG2_E = 1.4426950408889634
Q_FOLD = QK_SCALE * LOG2_E

Z_WIDTH = 3 * CONV_WIDTH + Q_RANK + KV_RANK + 4 * MLSTM_WIDTH + LANES
GATE_LANE = QK_ROPE
ROUTER_LANE = N_GROUPS

VMEM_LIMIT = 56 * 1024 * 1024

TOKEN_TILE = 512
ATTN_TILE = 512
MLSTM_CHUNK = 256
COMBINE_TILE = 256
PROMPT_EXPERT_BLOCK = 256
SAMPLE_EXPERT_BLOCK = 16
PAGES_PER_CHUNK = 64
STEP_SEQS = 1


def _cparams(n_grid, row_dma=False):
    return pltpu.CompilerParams(dimension_semantics=("arbitrary",) * n_grid, vmem_limit_bytes=VMEM_LIMIT,
                                disable_bounds_checks=row_dma)


def _layer_norm(x, g, b):
    mu = jnp.mean(x, -1, keepdims=True)
    xc = x - mu
    var = jnp.mean(xc * xc, -1, keepdims=True)
    return xc * lax.rsqrt(var + LN_EPS) * g + b


def _rms_norm(x, g):
    return x * lax.rsqrt(jnp.mean(x * x, -1, keepdims=True) + RMS_EPS) * g


def _log_sigmoid(x):
    return jnp.minimum(x, 0.0) - jnp.log1p(jnp.exp(-jnp.abs(x)))


def _sigmoid(x):
    return 1.0 / (1.0 + jnp.exp(-x))


def _mm(a, w):
    if w.dtype == BF16:
        return jnp.dot(a.astype(BF16), w, preferred_element_type=F32)
    return jnp.dot(a.astype(F32), w, precision=lax.Precision.HIGHEST, preferred_element_type=F32)


def _rope_block(x, cos_t, sin_t, first_lane):
    lane = lax.broadcasted_iota(jnp.int32, x.shape, 1)
    in_first = (lane >= first_lane) & (lane < first_lane + ROPE_HALF)
    partner = jnp.where(in_first, pltpu.roll(x, LANES - ROPE_HALF, 1), pltpu.roll(x, ROPE_HALF, 1))
    return x * cos_t + partner * sin_t


def _in_proj_kernel(*refs, apply_ln0, seq_mode, emit_kv, tm):
    it = iter(refs)
    x_ref = next(it)
    if apply_ln0:
        g0_ref, b0_ref = next(it), next(it)
    w_in_ref, convw_ref = next(it), next(it)
    if not seq_mode:
        hist0_ref, hist1_ref = next(it), next(it)
    qg_ref, wuq_ref, kvg_ref = next(it), next(it), next(it)
    if emit_kv:
        wuk_ref, wuv_ref = next(it), next(it)
    gb_ref, cq_ref, sq_ref, ck_ref, sk_ref = next(it), next(it), next(it), next(it), next(it)
    if apply_ln0:
        xn_ref = next(it)
    yconv_ref, q_ref, ckv_ref, krope_ref = next(it), next(it), next(it), next(it)
    mqT_ref, mk_ref, mvT_ref, og_ref, gates_ref, gatesT_ref = (next(it) for _ in range(6))
    if emit_kv:
        k_ref, v_ref = next(it), next(it)
    if seq_mode:
        cstate_ref = next(it)
        ubuf = next(it)
    else:
        u_ref = next(it)

    lane = lax.broadcasted_iota(jnp.int32, (tm, LANES), 1)
    x = x_ref[...]
    if apply_ln0:
        x = _layer_norm(x, g0_ref[...], b0_ref[...])
        xn_ref[...] = x
    z = _mm(x, w_in_ref[...])
    o = 0
    cb = z[:, o:o + CONV_WIDTH]; o += CONV_WIDTH
    cc = z[:, o:o + CONV_WIDTH]; o += CONV_WIDTH
    ch = z[:, o:o + CONV_WIDTH]; o += CONV_WIDTH
    c_q = z[:, o:o + Q_RANK]; o += Q_RANK
    c_kv = z[:, o:o + KV_RANK]; o += KV_RANK
    mq = z[:, o:o + MLSTM_WIDTH]; o += MLSTM_WIDTH
    mk = z[:, o:o + MLSTM_WIDTH]; o += MLSTM_WIDTH
    mv = z[:, o:o + MLSTM_WIDTH]; o += MLSTM_WIDTH
    mo = z[:, o:o + MLSTM_WIDTH]; o += MLSTM_WIDTH
    last = z[:, o:o + LANES]

    u = cc * ch
    w = convw_ref[...]
    if seq_mode:
        s_idx = pl.program_id(1)

        @pl.when(s_idx == 0)
        def _():
            ubuf[0:8, :] = jnp.zeros((8, CONV_WIDTH), F32)

        ubuf[8:8 + tm, :] = u
        conv = ubuf[6:6 + tm, :] * w[0:1] + ubuf[7:7 + tm, :] * w[1:2] + u * w[2:3]
        ubuf[0:8, :] = ubuf[tm:tm + 8, :]

        @pl.when(s_idx == pl.num_programs(1) - 1)
        def _():
            cstate_ref[...] = ubuf[6:8, :]
    else:
        conv = hist0_ref[...] * w[0:1] + hist1_ref[...] * w[1:2] + u * w[2:3]
        u_ref[...] = u
    yconv_ref[...] = cb * conv

    cqn = _rms_norm(c_q, qg_ref[...])
    q = _mm(cqn, wuq_ref[...])
    cq, sq = cq_ref[...], sq_ref[...]
    for h in range(MLA_HEADS):
        qh = _rope_block(q[:, h * LANES:(h + 1) * LANES], cq, sq, QK_NOPE) * Q_FOLD
        q_ref[:, h * LANES:(h + 1) * LANES] = qh.astype(q_ref.dtype)

    ckv = _rms_norm(c_kv, kvg_ref[...])
    ckv_ref[...] = ckv
    kr = _rope_block(last, ck_ref[...], sk_ref[...], 0)
    krope_ref[...] = kr[:, 0:QK_ROPE]
    if emit_kv:
        ckv_b = ckv.astype(BF16)
        kfull = jnp.dot(ckv_b, wuk_ref[...], preferred_element_type=F32)
        kr_shift = pltpu.roll(kr, QK_NOPE, 1)
        for h in range(MLA_HEADS):
            k_ref[:, h * LANES:(h + 1) * LANES] = (kfull[:, h * LANES:(h + 1) * LANES] + kr_shift).astype(BF16)
        vfull = jnp.dot(ckv_b, wuv_ref[...], preferred_element_type=F32)
        ones_hi = jnp.where(lane >= V_DIM, 1.0, 0.0)
        for h in range(MLA_HEADS):
            v_ref[:, h * LANES:(h + 1) * LANES] = (vfull[:, h * LANES:(h + 1) * LANES] + ones_hi).astype(BF16)

    mqT_ref[...] = mq.T.astype(mqT_ref.dtype)
    mk_ref[...] = (mk * (MLSTM_DH ** -0.5)).astype(mk_ref.dtype)
    mvT_ref[...] = mv.T.astype(mvT_ref.dtype)
    og_ref[...] = mo
    g = last + gb_ref[...]
    is_forget = (lane >= GATE_LANE + MLSTM_HEADS) & (lane < GATE_LANE + 2 * MLSTM_HEADS)
    g = jnp.where(is_forget, _log_sigmoid(g), g)
    gates_ref[...] = g[:, GATE_LANE:GATE_LANE + 2 * MLSTM_HEADS]
    gatesT_ref[...] = g.T[GATE_LANE:GATE_LANE + 2 * MLSTM_HEADS, :]


def _in_proj(x, lw, tabs, *, apply_ln0, ln0, seq_mode, emit_kv, hist=None):
    B, S, D = x.shape
    tm = min(TOKEN_TILE, S)
    assert S % tm == 0
    ns = S // tm
    tok = lambda w: pl.BlockSpec((None, tm, w), lambda b, s: (b, s, 0))
    full2 = lambda a: pl.BlockSpec(a.shape, lambda b, s: (0, 0))
    tab = pl.BlockSpec((tm, LANES), lambda b, s: (s, 0))

    ins, specs = [x], [tok(D)]
    if apply_ln0:
        ins += [ln0[0], ln0[1]]; specs += [full2(ln0[0]), full2(ln0[1])]
    ins += [lw["w_in"], lw["conv_w"]]; specs += [full2(lw["w_in"]), full2(lw["conv_w"])]
    if not seq_mode:
        ins += [hist[0], hist[1]]; specs += [tok(CONV_WIDTH), tok(CONV_WIDTH)]
    ins += [lw["q_norm_g"], lw["w_uq"], lw["kv_norm_g"]]
    specs += [full2(lw["q_norm_g"]), full2(lw["w_uq"]), full2(lw["kv_norm_g"])]
    if emit_kv:
        ins += [lw["w_uk"], lw["w_uv"]]; specs += [full2(lw["w_uk"]), full2(lw["w_uv"])]
    ins += [lw["gate_b"]] + list(tabs)
    specs += [full2(lw["gate_b"])] + [tab] * 4

    outs, ospecs = [], []

    def add(shape, dtype, spec):
        outs.append(jax.ShapeDtypeStruct(shape, dtype)); ospecs.append(spec)

    if apply_ln0:
        add((B, S, D), F32, tok(D))
    add((B, S, CONV_WIDTH), F32, tok(CONV_WIDTH))
    add((B, S, MLA_HEADS * LANES), BF16 if emit_kv else F32, tok(MLA_HEADS * LANES))
    add((B, S, KV_RANK), F32, tok(KV_RANK))
    add((B, S, QK_ROPE), F32, tok(QK_ROPE))
    mdt = BF16 if seq_mode else F32
    tokT = pl.BlockSpec((None, MLSTM_WIDTH, tm), lambda b, s: (b, 0, s))
    add((B, MLSTM_WIDTH, S), mdt, tokT)
    add((B, S, MLSTM_WIDTH), mdt, tok(MLSTM_WIDTH))
    add((B, MLSTM_WIDTH, S), mdt, tokT)
    add((B, S, MLSTM_WIDTH), F32, tok(MLSTM_WIDTH))
    add((B, S, 2 * MLSTM_HEADS), F32, tok(2 * MLSTM_HEADS))
    add((B, 2 * MLSTM_HEADS, S), F32, pl.BlockSpec((None, 2 * MLSTM_HEADS, tm), lambda b, s: (b, 0, s)))
    if emit_kv:
        add((B, S, MLA_HEADS * LANES), BF16, tok(MLA_HEADS * LANES))
        add((B, S, MLA_HEADS * LANES), BF16, tok(MLA_HEADS * LANES))
    scratch = []
    if seq_mode:
        add((B, CONV_K - 1, CONV_WIDTH), F32, pl.BlockSpec((None, CONV_K - 1, CONV_WIDTH), lambda b, s: (b, 0, 0)))
        scratch = [pltpu.VMEM((tm + 8, CONV_WIDTH), F32)]
    else:
        add((B, S, CONV_WIDTH), F32, tok(CONV_WIDTH))
    kern = functools.partial(_in_proj_kernel, apply_ln0=apply_ln0, seq_mode=seq_mode, emit_kv=emit_kv, tm=tm)
    return pl.pallas_call(kern, grid=(B, ns), in_specs=specs, out_specs=ospecs, out_shape=outs,
                          scratch_shapes=scratch, compiler_params=_cparams(2), name="in_proj")(*ins)


def _flash_attn_kernel(q_ref, k_ref, v_ref, o_ref, *, tq):
    qi = pl.program_id(2)
    row = lax.broadcasted_iota(jnp.int32, (tq, tq), 0)
    col = lax.broadcasted_iota(jnp.int32, (tq, tq), 1)
    causal = col <= row

    def block(ki, carry, masked):
        start = pl.multiple_of(ki * tq, tq)
        new = []
        for hh in range(2):
            m, acc = carry[hh]
            hl = slice(hh * LANES, (hh + 1) * LANES)
            k = k_ref[pl.ds(start, tq), hl]
            v = v_ref[pl.ds(start, tq), hl]
            s = lax.dot_general(q_ref[:, hl], k, (((1,), (1,)), ((), ())), preferred_element_type=F32)
            if masked:
                s = jnp.where(causal, s, -jnp.inf)
            m_new = jnp.maximum(m, jnp.max(s, -1, keepdims=True))
            p = jnp.exp2(s - m_new)
            acc = jnp.exp2(m - m_new) * acc + jnp.dot(p.astype(BF16), v, preferred_element_type=F32)
            new.append((m_new, acc))
        return tuple(new)

    init = ((jnp.full((tq, 1), -jnp.inf, F32), jnp.zeros((tq, LANES), F32)),) * 2

    def two_blocks(j, carry):
        return block(2 * j + 1, block(2 * j, carry, masked=False), masked=False)

    carry = lax.fori_loop(0, qi // 2, two_blocks, init)
    carry = lax.cond(qi % 2 == 1, lambda c: block(qi - 1, c, masked=False), lambda c: c, carry)
    fin = block(qi, carry, masked=True)
    outs = [acc / pltpu.roll(acc, V_DIM, 1) for _, acc in fin]
    lane = lax.broadcasted_iota(jnp.int32, (tq, LANES), 1)
    o_ref[...] = jnp.where(lane < V_DIM, outs[0], pltpu.roll(outs[1], V_DIM, 1))


def _flash_attn(q, k, v):
    B, S, _ = q.shape
    tq = min(ATTN_TILE, S)
    assert S % tq == 0
    kern = functools.partial(_flash_attn_kernel, tq=tq)
    return pl.pallas_call(
        kern, grid=(B, MLA_HEADS // 2, S // tq),
        in_specs=[pl.BlockSpec((None, tq, 2 * LANES), lambda b, h, i: (b, i, h)),
                  pl.BlockSpec((None, S, 2 * LANES), lambda b, h, i: (b, 0, h)),
                  pl.BlockSpec((None, S, 2 * LANES), lambda b, h, i: (b, 0, h))],
        out_specs=pl.BlockSpec((None, tq, 2 * V_DIM), lambda b, h, i: (b, i, h)),
        out_shape=jax.ShapeDtypeStruct((B, S, MLA_HEADS * V_DIM), F32),
        compiler_params=_cparams(3), name="flash_attn")(q, k, v)


def _paged_attn_kernel(pt_ref, q_ref, ckvn_ref, krn_ref, wukT_ref, wuv_ref, cckv_ref, ckrT_ref, o_ref,
                       kbuf, rbuf, sems, qlat_scr, qr_scr, m_scr, l_scr, acc_scr, *, layer, npc, nchunks, page):
    b, c = pl.program_id(0), pl.program_id(1)
    g = b * nchunks + c
    total = pl.num_programs(0) * nchunks
    slot = g % 2

    def page_copies(bb, cc, sl, j):
        pg = pt_ref[bb, cc * npc + j]
        return (pltpu.make_async_copy(cckv_ref.at[layer, pg], kbuf.at[sl, j], sems.at[0, sl]),
                pltpu.make_async_copy(ckrT_ref.at[layer, pg], rbuf.at[sl, j], sems.at[1, sl]))

    def start_chunk(bb, cc, sl):
        def issue(j, _):
            for cp in page_copies(bb, cc, sl, j):
                cp.start()
            return 0
        lax.fori_loop(0, npc, issue, 0, unroll=4)

    @pl.when(g == 0)
    def _():
        start_chunk(b, c, slot)

    @pl.when(g + 1 < total)
    def _():
        nxt = g + 1
        start_chunk(nxt // nchunks, nxt % nchunks, 1 - slot)

    @pl.when(c == 0)
    def _():
        qrow = q_ref[0]
        for h in range(MLA_HEADS):
            qn = jnp.broadcast_to(qrow[:, h * LANES:h * LANES + QK_NOPE], (8, QK_NOPE)).astype(BF16)
            qlat_scr[h:h + 1, :] = jnp.dot(qn, wukT_ref[h], preferred_element_type=F32)[0:1]
            qr_scr[h:h + 1, :] = qrow[:, h * LANES + QK_NOPE:h * LANES + QK_NOPE + QK_ROPE]
        m_scr[...] = jnp.full(m_scr.shape, -jnp.inf, F32)
        l_scr[...] = jnp.zeros(l_scr.shape, F32)
        acc_scr[...] = jnp.zeros(acc_scr.shape, F32)

    pltpu.make_async_copy(cckv_ref.at[layer, pl.ds(0, npc)], kbuf.at[slot], sems.at[0, slot]).wait()
    pltpu.make_async_copy(ckrT_ref.at[layer, pl.ds(0, npc)], rbuf.at[slot], sems.at[1, slot]).wait()

    kb = kbuf[slot].reshape(npc * page, KV_RANK).astype(BF16)
    qr = qr_scr[...].astype(BF16)
    s_rope = jnp.concatenate([jnp.dot(qr, rbuf[slot, j].astype(BF16), preferred_element_type=F32)
                              for j in range(npc)], axis=-1)
    s = lax.dot_general(qlat_scr[...].astype(BF16), kb, (((1,), (1,)), ((), ())),
                        preferred_element_type=F32) + s_rope
    m = m_scr[...]
    m_new = jnp.maximum(m, jnp.max(s, -1, keepdims=True))
    alpha = jnp.exp2(m - m_new)
    p = jnp.exp2(s - m_new)
    l_scr[...] = alpha * l_scr[...] + jnp.sum(p, -1, keepdims=True)
    acc_scr[...] = alpha * acc_scr[...] + jnp.dot(p.astype(BF16), kb, preferred_element_type=F32)
    m_scr[...] = m_new

    @pl.when(c == nchunks - 1)
    def _():
        ckv_new, kr_new = ckvn_ref[0], krn_ref[0]
        s_new = (jnp.sum(qlat_scr[...] * ckv_new, -1, keepdims=True)
                 + jnp.sum(qr_scr[...] * kr_new, -1, keepdims=True))
        m_old = m_scr[...]
        m_fin = jnp.maximum(m_old, s_new)
        a = jnp.exp2(m_old - m_fin)
        p_new = jnp.exp2(s_new - m_fin)
        l_fin = a * l_scr[...] + p_new
        o_lat = (a * acc_scr[...] + p_new * ckv_new) / l_fin
        for h in range(MLA_HEADS):
            oh = jnp.broadcast_to(o_lat[h:h + 1, :], (8, KV_RANK)).astype(BF16)
            o_ref[0, :, h * V_DIM:(h + 1) * V_DIM] = jnp.dot(oh, wuv_ref[h], preferred_element_type=F32)[0:1]


def _paged_attn(q, ckv_new, kr_new, wukT, wuv_h, cache_ckv, cache_krope_t, page_table, layer):
    DB = q.shape[0]
    n_pages = page_table.shape[1]
    page = cache_ckv.shape[2]
    npc = min(PAGES_PER_CHUNK, n_pages)
    assert n_pages % npc == 0
    nchunks = n_pages // npc
    kern = functools.partial(_paged_attn_kernel, layer=layer, npc=npc, nchunks=nchunks, page=page)
    row = lambda w: pl.BlockSpec((1, 1, w), lambda b, c, pt: (b, 0, 0))
    full3 = lambda a: pl.BlockSpec(a.shape, lambda b, c, pt: (0, 0, 0))
    gs = pltpu.PrefetchScalarGridSpec(
        num_scalar_prefetch=1, grid=(DB, nchunks),
        in_specs=[row(MLA_HEADS * LANES), row(KV_RANK), row(QK_ROPE), full3(wukT), full3(wuv_h),
                  pl.BlockSpec(memory_space=pl.ANY), pl.BlockSpec(memory_space=pl.ANY)],
        out_specs=row(MLA_HEADS * V_DIM),
        scratch_shapes=[pltpu.VMEM((2, npc, page, KV_RANK), F32), pltpu.VMEM((2, npc, QK_ROPE, page), F32),
                        pltpu.SemaphoreType.DMA((2, 2)),
                        pltpu.VMEM((MLA_HEADS, KV_RANK), F32), pltpu.VMEM((MLA_HEADS, QK_ROPE), F32),
                        pltpu.VMEM((MLA_HEADS, 1), F32), pltpu.VMEM((MLA_HEADS, 1), F32),
                        pltpu.VMEM((MLA_HEADS, KV_RANK), F32)])
    return pl.pallas_call(kern, grid_spec=gs, out_shape=jax.ShapeDtypeStruct((DB, 1, MLA_HEADS * V_DIM), F32),
                          compiler_params=_cparams(2, row_dma=True), name="paged_attn")(
        page_table, q, ckv_new, kr_new, wukT, wuv_h, cache_ckv, cache_krope_t)


def _mlstm_seq_kernel(qT_ref, k_ref, vT_ref, g_ref, gT_ref, h_ref, c_out, n_out, m_out, c_scr, n_scr, m_scr, hT_scr, *, L):
    c = pl.program_id(1)

    @pl.when(c == 0)
    def _():
        c_scr[...] = jnp.zeros(c_scr.shape, F32)
        n_scr[...] = jnp.zeros(n_scr.shape, F32)
        m_scr[...] = jnp.zeros(m_scr.shape, F32)

    g = g_ref[...]
    gT = gT_ref[...]
    src = lax.broadcasted_iota(jnp.int32, (L, L), 0)
    tgt = lax.broadcasted_iota(jnp.int32, (L, L), 1)
    causal = src <= tgt
    b_cols = jnp.dot((tgt <= src).astype(F32), g, precision=lax.Precision.HIGHEST, preferred_element_type=F32)
    b_rows = jnp.dot(gT, causal.astype(F32), precision=lax.Precision.HIGHEST, preferred_element_type=F32)
    for h in range(MLSTM_HEADS):
        sl = slice(h * MLSTM_DH, (h + 1) * MLSTM_DH)
        qT, kh, vT = qT_ref[sl, :], k_ref[:, sl], vT_ref[sl, :]
        a_col = g[:, h:h + 1] - b_cols[:, MLSTM_HEADS + h:MLSTM_HEADS + h + 1]
        b_row = b_rows[MLSTM_HEADS + h:MLSTM_HEADS + h + 1, :]
        i_row = gT[h:h + 1, :]
        m_prev = m_scr[h:h + 1, 0:1]
        log_d = jnp.where(causal, a_col + b_row, -jnp.inf)
        m_inter = b_row + m_prev
        m_t = jnp.maximum(m_inter, jnp.max(log_d, 0, keepdims=True))
        w_inter = jnp.exp(m_inter - m_t)
        sT = jnp.dot(kh, qT, preferred_element_type=F32) * jnp.exp(log_d - m_t)
        C = c_scr[h]
        n8 = jnp.broadcast_to(n_scr[h:h + 1, :], (8, MLSTM_DH)).astype(BF16)
        numT = (jnp.dot(vT, sT.astype(BF16), preferred_element_type=F32)
                + w_inter * jnp.dot(C.astype(BF16), qT, preferred_element_type=F32))
        den = jnp.sum(sT, 0, keepdims=True) + w_inter * jnp.dot(n8, qT, preferred_element_type=F32)[0:1]
        hT_scr[sl, :] = numT / jnp.maximum(jnp.abs(den), jnp.exp(-m_t))
        m_new = m_t[:, L - 1:L]
        b_last = b_row[:, L - 1:L]
        w_old = jnp.exp(b_last + m_prev - m_new)
        w_s = jnp.exp(b_last - b_row + i_row - m_new)
        c_scr[h] = w_old * C + jnp.dot((vT.astype(F32) * w_s).astype(BF16), kh, preferred_element_type=F32)
        ws8 = jnp.broadcast_to(w_s, (8, L)).astype(BF16)
        n_scr[h:h + 1, :] = w_old * n_scr[h:h + 1, :] + jnp.dot(ws8, kh, preferred_element_type=F32)[0:1]
        m_scr[h:h + 1, :] = jnp.broadcast_to(m_new, (1, LANES))
    h_ref[...] = hT_scr[...].T

    @pl.when(c == pl.num_programs(1) - 1)
    def _():
        c_out[...] = c_scr[...]
        n_out[...] = n_scr[...]
        m_out[...] = m_scr[...]


def _eye(n):
    return (lax.broadcasted_iota(jnp.int32, (n, n), 0) == lax.broadcasted_iota(jnp.int32, (n, n), 1)).astype(F32)


def _col_to_row(col):
    return jnp.sum(_eye(col.shape[0]) * col, 0, keepdims=True)


def _row_to_col(row):
    return jnp.sum(_eye(row.shape[1]) * row, 1, keepdims=True)


def _mlstm_seq(mqT, mk, mvT, gates, gatesT):
    B, S, _ = mk.shape
    L = min(MLSTM_CHUNK, S)
    assert S % L == 0
    kern = functools.partial(_mlstm_seq_kernel, L=L)
    tok = lambda w: pl.BlockSpec((None, L, w), lambda b, c: (b, c, 0))
    tokT = lambda w: pl.BlockSpec((None, w, L), lambda b, c: (b, 0, c))
    st = lambda *dims: pl.BlockSpec((None,) + dims, lambda b, c: (b,) + (0,) * len(dims))
    return pl.pallas_call(
        kern, grid=(B, S // L),
        in_specs=[tokT(MLSTM_WIDTH), tok(MLSTM_WIDTH), tokT(MLSTM_WIDTH), tok(2 * MLSTM_HEADS), tokT(2 * MLSTM_HEADS)],
        out_specs=[tok(MLSTM_WIDTH), st(MLSTM_HEADS, MLSTM_DH, MLSTM_DH), st(MLSTM_HEADS, MLSTM_DH), st(8, LANES)],
        out_shape=[jax.ShapeDtypeStruct((B, S, MLSTM_WIDTH), F32),
                   jax.ShapeDtypeStruct((B, MLSTM_HEADS, MLSTM_DH, MLSTM_DH), F32),
                   jax.ShapeDtypeStruct((B, MLSTM_HEADS, MLSTM_DH), F32),
                   jax.ShapeDtypeStruct((B, 8, LANES), F32)],
        scratch_shapes=[pltpu.VMEM((MLSTM_HEADS, MLSTM_DH, MLSTM_DH), F32), pltpu.VMEM((MLSTM_HEADS, MLSTM_DH), F32),
                        pltpu.VMEM((8, LANES), F32), pltpu.VMEM((MLSTM_WIDTH, L), F32)],
        compiler_params=_cparams(2), name="mlstm_seq")(mqT, mk, mvT, gates, gatesT)


def _mlstm_step_kernel(q_ref, k_ref, v_ref, g_ref, c_ref, n_ref, m_ref, h_ref, c_out, n_out, m_out):
    for i in range(q_ref.shape[0]):
        g = g_ref[i]
        m_all = m_ref[i]
        for h in range(MLSTM_HEADS):
            sl = slice(h * MLSTM_DH, (h + 1) * MLSTM_DH)
            q, k, v = q_ref[i][:, sl], k_ref[i][:, sl], v_ref[i][:, sl]
            C, n = c_ref[i, h], n_ref[i, h:h + 1, :]
            ig, lf = g[:, h:h + 1], g[:, MLSTM_HEADS + h:MLSTM_HEADS + h + 1]
            m_prev = m_all[:, h:h + 1]
            m_inter = lf + m_prev
            m_t = jnp.maximum(m_inter, ig)
            d = jnp.exp(ig - m_t)
            w_inter = jnp.exp(m_inter - m_t)
            s = jnp.sum(q * k, -1, keepdims=True) * d
            cq = jnp.sum(C * q, -1, keepdims=True)
            v_col = _row_to_col(v)
            num = s * v_col + w_inter * cq
            den = s + w_inter * jnp.sum(n * q, -1, keepdims=True)
            h_col = num / jnp.maximum(jnp.abs(den), jnp.exp(-m_t))
            h_ref[i, :, sl] = _col_to_row(h_col)
            c_out[i, h] = w_inter * C + d * (v_col * k)
            n_out[i, h:h + 1, :] = w_inter * n + d * k
            m_out[i, :, h:h + 1] = m_t


def _mlstm_step(mq, mk, mv, gates, C0, n0, m0):
    DB = mq.shape[0]
    sb = STEP_SEQS if DB % STEP_SEQS == 0 else 1
    r3 = lambda a: a.reshape(DB, 1, a.shape[-1])
    row = lambda w: pl.BlockSpec((sb, 1, w), lambda b: (b, 0, 0))
    cspec = pl.BlockSpec((sb, MLSTM_HEADS, MLSTM_DH, MLSTM_DH), lambda b: (b, 0, 0, 0))
    nspec = pl.BlockSpec((sb, MLSTM_HEADS, MLSTM_DH), lambda b: (b, 0, 0))
    return pl.pallas_call(
        _mlstm_step_kernel, grid=(DB // sb,),
        in_specs=[row(MLSTM_WIDTH)] * 3 + [row(2 * MLSTM_HEADS), cspec, nspec, row(MLSTM_HEADS)],
        out_specs=[row(MLSTM_WIDTH), cspec, nspec, row(MLSTM_HEADS)],
        out_shape=[jax.ShapeDtypeStruct((DB, 1, MLSTM_WIDTH), F32), jax.ShapeDtypeStruct(C0.shape, F32),
                   jax.ShapeDtypeStruct(n0.shape, F32), jax.ShapeDtypeStruct((DB, 1, MLSTM_HEADS), F32)],
        compiler_params=_cparams(1), name="mlstm_step")(r3(mq), r3(mk), r3(mv), r3(gates), C0, n0, r3(m0))


def _mix_out_kernel(yc_ref, att_ref, hm_ref, og_ref, x_ref, mg_ref, wout_ref, g1_ref, b1_ref, wr_ref, br_ref,
                    h1_ref, route_ref, cnt_ref, ymix_scr, carry_scr, *, tm, alpha):
    @pl.when(pl.program_id(0) == 0)
    def _():
        carry_scr[...] = jnp.zeros(carry_scr.shape, F32)

    lane = lax.broadcasted_iota(jnp.int32, (tm, LANES), 1)
    low = lane < HEAD_DIM

    def norm_store(y, j):
        y2 = y * y
        s_lo = jnp.sum(jnp.where(low, y2, 0.0), -1, keepdims=True)
        s_hi = jnp.sum(jnp.where(low, 0.0, y2), -1, keepdims=True)
        inv = jnp.where(low, lax.rsqrt(s_lo / HEAD_DIM + RMS_EPS), lax.rsqrt(s_hi / HEAD_DIM + RMS_EPS))
        ymix_scr[:, j * LANES:(j + 1) * LANES] = (y * inv * mg_ref[:, j * LANES:(j + 1) * LANES]).astype(ymix_scr.dtype)

    j = 0
    for src, width in ((yc_ref, CONV_WIDTH), (att_ref, MLA_HEADS * V_DIM)):
        for t in range(width // LANES):
            norm_store(src[:, t * LANES:(t + 1) * LANES], j)
            j += 1
    for t in range(MLSTM_WIDTH // LANES):
        sl = slice(t * LANES, (t + 1) * LANES)
        norm_store(_sigmoid(og_ref[:, sl]) * hm_ref[:, sl], j)
        j += 1

    proj = _mm(ymix_scr[...], wout_ref[...])
    h1 = _layer_norm(alpha * x_ref[...] + proj, g1_ref[...], b1_ref[...])
    h1_ref[...] = h1

    h_hi = h1.astype(BF16)
    h_lo = (h1 - h_hi.astype(F32)).astype(BF16)
    part = jnp.dot(h_hi, wr_ref[...], preferred_element_type=F32)
    logits = (part[:, 0:LANES] + part[:, LANES:2 * LANES]
              + jnp.dot(h_lo, wr_ref[:, 0:LANES], preferred_element_type=F32) + br_ref[...])
    lanef = lane.astype(F32)
    big = float(LANES)
    neg = -jnp.inf
    is_g = lane < N_GROUPS
    gl = jnp.where(is_g, logits, neg)
    gmax = jnp.max(gl, -1, keepdims=True)
    g_sel = jnp.min(jnp.where(gl == gmax, lanef, big), -1, keepdims=True)
    g_w = 1.0 / jnp.sum(jnp.where(is_g, jnp.exp(logits - gmax), 0.0), -1, keepdims=True)
    lo = ROUTER_LANE + EXPERTS_PER_GROUP * g_sel
    in_grp = (lanef >= lo) & (lanef < lo + EXPERTS_PER_GROUP)
    el = jnp.where(in_grp, logits, neg)
    e1 = jnp.max(el, -1, keepdims=True)
    i1 = jnp.min(jnp.where(el == e1, lanef, big), -1, keepdims=True)
    el2 = jnp.where(lanef == i1, neg, el)
    e2 = jnp.max(el2, -1, keepdims=True)
    i2 = jnp.min(jnp.where(el2 == e2, lanef, big), -1, keepdims=True)
    zsum = jnp.sum(jnp.where(in_grp, jnp.exp(logits - e1), 0.0), -1, keepdims=True)
    p1 = 1.0 / zsum
    p2 = jnp.exp(e2 - e1) / zsum
    psum = p1 + p2
    w1 = p1 / psum * g_w
    w2 = p2 / psum * g_w

    oh1 = (lanef == i1).astype(F32)
    oh2 = (lanef == i2).astype(F32)
    both = oh1 + oh2
    r = lax.broadcasted_iota(jnp.int32, (tm, tm), 0)
    c = lax.broadcasted_iota(jnp.int32, (tm, tm), 1)
    before = jnp.dot((c < r).astype(BF16), both.astype(BF16), preferred_element_type=F32) + carry_scr[0:1, :]
    r1 = jnp.sum(oh1 * before, -1, keepdims=True)
    r2 = jnp.sum(oh2 * before, -1, keepdims=True)
    carry = carry_scr[0:1, :] + jnp.sum(both, 0, keepdims=True)
    carry_scr[...] = jnp.broadcast_to(carry, carry_scr.shape)
    cnt_ref[...] = jnp.broadcast_to(carry, cnt_ref.shape)

    vals = (i1 - ROUTER_LANE, i2 - ROUTER_LANE, r1, r2, w1, w2)
    packed = jnp.zeros((tm, LANES), F32)
    for idx, val in enumerate(vals):
        packed = jnp.where(lane == idx, val, packed)
    route_ref[...] = packed[:, 0:8]


def _mix_out(yconv, att, hm, og, x, lw, alpha):
    T, D = x.shape
    tm = min(TOKEN_TILE, T)
    assert T % tm == 0
    tok = lambda w: pl.BlockSpec((tm, w), lambda i: (i, 0))
    full = lambda a: pl.BlockSpec(a.shape, lambda i: (0, 0))
    ws = [lw["mix_norm_g"], lw["w_out"], lw["ln1_g"], lw["ln1_b"], lw["w_router"], lw["b_router"]]
    kern = functools.partial(_mix_out_kernel, tm=tm, alpha=alpha)
    return pl.pallas_call(
        kern, grid=(T // tm,),
        in_specs=[tok(CONV_WIDTH), tok(MLA_HEADS * V_DIM), tok(MLSTM_WIDTH), tok(MLSTM_WIDTH), tok(D)] + [full(a) for a in ws],
        out_specs=[tok(D), tok(8), pl.BlockSpec((8, LANES), lambda i: (0, 0))],
        out_shape=[jax.ShapeDtypeStruct((T, D), F32), jax.ShapeDtypeStruct((T, 8), F32),
                   jax.ShapeDtypeStruct((8, LANES), F32)],
        scratch_shapes=[pltpu.VMEM((tm, D), lw["w_out"].dtype), pltpu.VMEM((8, LANES), F32)],
        compiler_params=_cparams(1), name="mix_out")(yconv, att, hm, og, x, *ws)


def _ple_dispatch_kernel(dest_ref, h1_ref, p_ref, wpe_ref, wpg_ref, xs_in, ple_ref, xs_out, sem, *, tm):
    del xs_in

    for r in range(tm):
        for k in range(TOP_K):
            pltpu.make_async_copy(h1_ref.at[pl.ds(r, 1)], xs_out.at[pl.ds(dest_ref[TOP_K * r + k], 1)], sem).start()

    ple_ref[...] = _mm(p_ref[...], wpe_ref[...]) * _sigmoid(_mm(h1_ref[...], wpg_ref[...]))

    for k in range(TOP_K):
        pltpu.make_async_copy(h1_ref, xs_out.at[pl.ds(0, tm)], sem).wait()


def _ple_dispatch(dest_flat, h1, p, lw, n_rows):
    T, D = h1.shape
    tm = min(TOKEN_TILE, T)
    tok = lambda w: pl.BlockSpec((tm, w), lambda i: (i, 0))
    full = lambda a: pl.BlockSpec(a.shape, lambda i: (0, 0))
    xs0 = jnp.zeros((n_rows, D), F32)
    kern = functools.partial(_ple_dispatch_kernel, tm=tm)
    ple, xs = pl.pallas_call(
        kern, grid=(T // tm,),
        in_specs=[pl.BlockSpec((TOP_K * tm,), lambda i: (i,), memory_space=pltpu.SMEM),
                  tok(D), pl.BlockSpec((None, tm, p.shape[2]), lambda i: (lw["layer"], i, 0)),
                  full(lw["w_pe"]), full(lw["w_pg"]), pl.BlockSpec(memory_space=pl.ANY)],
        out_specs=[tok(D), pl.BlockSpec(memory_space=pl.ANY)],
        out_shape=[jax.ShapeDtypeStruct((T, D), F32), jax.ShapeDtypeStruct((n_rows, D), F32)],
        scratch_shapes=[pltpu.SemaphoreType.DMA(())],
        input_output_aliases={5: 1},
        compiler_params=_cparams(1, row_dma=True), name="ple_dispatch")(dest_flat, h1, p, lw["w_pe"], lw["w_pg"], xs0)
    return ple, xs


def _experts_kernel(be_ref, nu_ref, xs_ref, wg_ref, wu_ref, wd_ref, ys_ref):
    del be_ref
    i = pl.program_id(0)

    @pl.when(i < nu_ref[0])
    def _():
        x = xs_ref[...]
        gt, up = _mm(x, wg_ref[0]), _mm(x, wu_ref[0])
        ys_ref[...] = _mm(gt * _sigmoid(gt) * up, wd_ref[0])

    @pl.when(i >= nu_ref[0])
    def _():
        ys_ref[...] = jnp.zeros(ys_ref.shape, F32)


def _experts(blk_expert, n_used, xs, lw, blk):
    n_rows, D = xs.shape
    layer = lw["layer"]
    wspec = lambda a: pl.BlockSpec((None, 1) + a.shape[2:], lambda i, be, nu: (layer, be[i], 0, 0))
    gs = pltpu.PrefetchScalarGridSpec(
        num_scalar_prefetch=2, grid=(n_rows // blk,),
        in_specs=[pl.BlockSpec((blk, D), lambda i, be, nu: (i, 0)),
                  wspec(lw["e_gate"]), wspec(lw["e_up"]), wspec(lw["e_down"])],
        out_specs=pl.BlockSpec((blk, D), lambda i, be, nu: (i, 0)))
    return pl.pallas_call(_experts_kernel, grid_spec=gs, out_shape=jax.ShapeDtypeStruct((n_rows, D), F32),
                          compiler_params=_cparams(1), name="experts")(
        blk_expert, n_used, xs, lw["e_gate"], lw["e_up"], lw["e_down"])


def _combine_kernel(dest_ref, dnext_ref, h1_ref, ple_ref, route_ref, g2_ref, b2_ref, ys_hbm, out_ref, gbuf, sems,
                    *, tm, alpha):
    i = pl.program_id(0)
    slot = i % 2

    def start_gather(d_ref, sl):
        def issue(r, _):
            for k in range(TOP_K):
                pltpu.make_async_copy(ys_hbm.at[pl.ds(d_ref[TOP_K * r + k], 1)], gbuf.at[sl, k, pl.ds(r, 1)],
                                      sems.at[sl]).start()
            return 0
        lax.fori_loop(0, tm, issue, 0, unroll=8)

    @pl.when(i == 0)
    def _():
        start_gather(dest_ref, slot)

    @pl.when(i + 1 < pl.num_programs(0))
    def _():
        start_gather(dnext_ref, 1 - slot)

    for k in range(TOP_K):
        pltpu.make_async_copy(ys_hbm.at[pl.ds(0, tm)], gbuf.at[slot, k], sems.at[slot]).wait()

    route = route_ref[...]
    moe = gbuf[slot, 0] * route[:, 4:5] + gbuf[slot, 1] * route[:, 5:6]
    out_ref[...] = _layer_norm(alpha * h1_ref[...] + moe + ple_ref[...], g2_ref[...], b2_ref[...])


def _combine(dest_flat, h1, ple, route, ys, lw, alpha):
    T, D = h1.shape
    tm = min(COMBINE_TILE, T)
    tok = lambda w: pl.BlockSpec((tm, w), lambda i: (i, 0))
    full = lambda a: pl.BlockSpec(a.shape, lambda i: (0, 0))
    kern = functools.partial(_combine_kernel, tm=tm, alpha=alpha)
    last = T // tm - 1
    return pl.pallas_call(
        kern, grid=(T // tm,),
        in_specs=[pl.BlockSpec((TOP_K * tm,), lambda i: (i,), memory_space=pltpu.SMEM),
                  pl.BlockSpec((TOP_K * tm,), lambda i: (jnp.minimum(i + 1, last),), memory_space=pltpu.SMEM),
                  tok(D), tok(D), tok(8), full(lw["ln2_g"]), full(lw["ln2_b"]), pl.BlockSpec(memory_space=pl.ANY)],
        out_specs=tok(D), out_shape=jax.ShapeDtypeStruct((T, D), F32),
        scratch_shapes=[pltpu.VMEM((2, TOP_K, tm, D), F32), pltpu.SemaphoreType.DMA((2,))],
        compiler_params=_cparams(1, row_dma=True), name="combine")(
        dest_flat, dest_flat, h1, ple, route, lw["ln2_g"], lw["ln2_b"], ys)


def _channel_mixer(yconv, att, hm, og, x, p, lw, alpha, blk):
    T, D = x.shape
    h1, route, cnt = _mix_out(yconv, att, hm, og, x, lw, alpha)
    counts = cnt[0, ROUTER_LANE:ROUTER_LANE + N_EXPERTS].astype(jnp.int32)
    padded = (counts + blk - 1) // blk * blk
    pad_end = jnp.cumsum(padded)
    pad_start = pad_end - padded
    n_blocks = -(-T * TOP_K // blk) + N_EXPERTS
    eid = route[:, 0:TOP_K].astype(jnp.int32)
    rank = route[:, TOP_K:2 * TOP_K].astype(jnp.int32)
    dest_flat = (pad_start[eid] + rank).reshape(T * TOP_K)
    blk_first_row = jnp.arange(n_blocks, dtype=jnp.int32) * blk
    blk_expert = jnp.minimum(jnp.sum((pad_end[None, :] <= blk_first_row[:, None]).astype(jnp.int32), -1), N_EXPERTS - 1)
    n_used = (pad_end[-1:] // blk).astype(jnp.int32)
    ple, xs = _ple_dispatch(dest_flat, h1, p, lw, n_blocks * blk)
    ys = _experts(blk_expert, n_used, xs, lw, blk)
    return _combine(dest_flat, h1, ple, route, ys, lw, alpha)


def _rope_tables(pos, n_rows):
    inv = ROPE_THETA ** (-jnp.arange(ROPE_HALF, dtype=F32) / ROPE_HALF)
    ang = pos.astype(F32)[:, None] * inv
    cos, sin = jnp.cos(ang), jnp.sin(ang)
    if cos.shape[0] != n_rows:
        cos, sin = jnp.broadcast_to(cos, (n_rows, ROPE_HALF)), jnp.broadcast_to(sin, (n_rows, ROPE_HALF))
    z = lambda w: jnp.zeros((n_rows, w), F32)
    one = jnp.ones((n_rows, QK_NOPE), F32)
    pad_q = LANES - QK_NOPE - QK_ROPE
    cq = jnp.concatenate([one, cos, cos, z(pad_q)], -1)
    sq = jnp.concatenate([z(QK_NOPE), -sin, sin, z(pad_q)], -1)
    ck = jnp.concatenate([cos, cos, z(LANES - QK_ROPE)], -1)
    sk = jnp.concatenate([-sin, sin, z(LANES - QK_ROPE)], -1)
    return cq, sq, ck, sk


def _prep_layer(i, w_in, conv_w, q_norm_g, w_uq, kv_norm_g, w_ukv, mlstm_gate_b, mix_norm_g, w_out, ln1_g, ln1_b,
                w_group, b_group, w_expert, b_expert, e_gate, e_up, e_down, w_pe, w_pg, ln2_g, ln2_b, e_bf16):
    D = w_in.shape[1]
    cuts, o = [], 0
    for wdt in [CONV_WIDTH] * 3 + [Q_RANK, KV_RANK, QK_ROPE] + [MLSTM_WIDTH] * 4 + [MLSTM_HEADS] * 2:
        cuts.append((o, o + wdt)); o += wdt
    wi = w_in[i]
    col = lambda j: wi[:, cuts[j][0]:cuts[j][1]]
    tail_pad = jnp.zeros((D, LANES - QK_ROPE - 2 * MLSTM_HEADS), F32)
    w_in_p = jnp.concatenate([col(0), col(1), col(2), col(3), col(4), col(6), col(7), col(8), col(9),
                              col(5), col(10), col(11), tail_pad], -1)
    wq = w_uq[i].reshape(Q_RANK, MLA_HEADS, QK_NOPE + QK_ROPE)
    w_uq_p = jnp.pad(wq, ((0, 0), (0, 0), (0, LANES - QK_NOPE - QK_ROPE))).reshape(Q_RANK, MLA_HEADS * LANES)
    wkv = w_ukv[i].reshape(KV_RANK, MLA_HEADS, QK_NOPE + V_DIM)
    w_uk, w_uv = wkv[..., :QK_NOPE], wkv[..., QK_NOPE:]
    w_uk_p = jnp.pad(w_uk, ((0, 0), (0, 0), (0, LANES - QK_NOPE))).reshape(KV_RANK, MLA_HEADS * LANES).astype(BF16)
    gate_b = jnp.concatenate([jnp.zeros((QK_ROPE,), F32), mlstm_gate_b[i],
                              jnp.zeros((LANES - QK_ROPE - 2 * MLSTM_HEADS,), F32)])[None, :]
    w_router = jnp.concatenate([w_group[i], w_expert[i], jnp.zeros((D, LANES - N_GROUPS - N_EXPERTS), F32)], -1)
    w_router_hi = w_router.astype(BF16)
    w_router = jnp.concatenate([w_router_hi, (w_router - w_router_hi.astype(F32)).astype(BF16)], -1)
    b_router = jnp.concatenate([b_group[i], b_expert[i], jnp.zeros((LANES - N_GROUPS - N_EXPERTS,), F32)])[None, :]
    e_gate_b, e_up_b, e_down_b = e_bf16
    lw = dict(
        layer=i, w_in=w_in_p.astype(BF16), conv_w=conv_w[i], q_norm_g=q_norm_g[i][None, :], w_uq=w_uq_p.astype(BF16),
        kv_norm_g=kv_norm_g[i][None, :], w_uk=w_uk_p,
        w_uv=jnp.pad(w_uv, ((0, 0), (0, 0), (0, LANES - V_DIM))).reshape(KV_RANK, MLA_HEADS * LANES).astype(BF16),
        w_ukT_h=jnp.transpose(w_uk, (1, 2, 0)).astype(BF16),
        w_uv_h=jnp.transpose(w_uv, (1, 0, 2)).astype(BF16),
        gate_b=gate_b, mix_norm_g=mix_norm_g[i][None, :], w_out=w_out[i].astype(BF16),
        ln1_g=ln1_g[i][None, :], ln1_b=ln1_b[i][None, :], w_router=w_router, b_router=b_router,
        e_gate=e_gate_b, e_up=e_up_b, e_down=e_down_b,
        w_pe=w_pe[i].astype(BF16), w_pg=w_pg[i].astype(BF16), ln2_g=ln2_g[i][None, :], ln2_b=ln2_b[i][None, :])
    lw_s = dict(lw, w_in=w_in_p, w_uq=w_uq_p, w_out=w_out[i], e_gate=e_gate, e_up=e_up, e_down=e_down,
                w_pe=w_pe[i], w_pg=w_pg[i])
    return lw, lw_s


def kernel(x_prompt, x_sample, cache_ckv, cache_krope, state_conv, state_mlstm_C, state_mlstm_n, state_mlstm_m,
           page_table, p_prompt, p_sample, ln0_g, ln0_b, w_in, conv_w, q_norm_g, w_uq, kv_norm_g, w_ukv,
           mlstm_gate_b, mix_norm_g, w_out, ln1_g, ln1_b, w_group, b_group, w_expert, b_expert,
           e_gate, e_up, e_down, w_pe, w_pg, ln2_g, ln2_b):
    B, S, D = x_prompt.shape
    DB, DS, _ = x_sample.shape
    assert DS == 1
    depth = w_in.shape[0]
    alpha = (2 * depth) ** 0.25
    past_len = page_table.shape[1] * cache_ckv.shape[2]
    ln0 = (ln0_g[None, :], ln0_b[None, :])
    tabs_p = _rope_tables(jnp.arange(S), S)
    tabs_s = _rope_tables(jnp.full((1,), past_len), DB)
    cache_krope_t = jnp.swapaxes(cache_krope, 2, 3)
    e_bf16 = (e_gate.astype(BF16), e_up.astype(BF16), e_down.astype(BF16))

    hp = x_prompt
    hs = x_sample.reshape(1, DB, D)
    st_p, st_s = [], []
    for i in range(depth):
        lw, lw_s = _prep_layer(i, w_in, conv_w, q_norm_g, w_uq, kv_norm_g, w_ukv, mlstm_gate_b, mix_norm_g, w_out,
                               ln1_g, ln1_b, w_group, b_group, w_expert, b_expert, e_gate, e_up, e_down, w_pe, w_pg,
                               ln2_g, ln2_b, e_bf16)
        first = i == 0
        outs = _in_proj(hp, lw, tabs_p, apply_ln0=first, ln0=ln0, seq_mode=True, emit_kv=True)
        if first:
            hp, outs = outs[0], outs[1:]
        yconv, q, ckv, krope, mqT, mk, mvT, og, gates, gatesT, k, v, conv_state = outs
        att = _flash_attn(q, k, v)
        hm, C1, n1, m1 = _mlstm_seq(mqT, mk, mvT, gates, gatesT)
        T = B * S
        flat = lambda a: a.reshape(T, a.shape[-1])
        hp = _channel_mixer(flat(yconv), flat(att), flat(hm), flat(og), flat(hp), p_prompt.reshape(depth, T, -1), lw, alpha,
                            PROMPT_EXPERT_BLOCK).reshape(B, S, D)
        st_p.append((ckv, krope, conv_state, C1, n1, m1[:, :MLSTM_HEADS, 0]))
        hist = (state_conv[i][None, :, 0, :], state_conv[i][None, :, 1, :])
        outs = _in_proj(hs, lw_s, tabs_s, apply_ln0=first, ln0=ln0, seq_mode=False, emit_kv=False, hist=hist)
        if first:
            hs, outs = outs[0], outs[1:]
        yconv, q, ckv, krope, mqT, mk, mvT, og, gates, gatesT, u = outs
        r3 = lambda a: a.reshape(DB, 1, a.shape[-1])
        att = _paged_attn(r3(q), r3(ckv), r3(krope), lw["w_ukT_h"], lw["w_uv_h"], cache_ckv, cache_krope_t,
                          page_table, i)
        hm, C1, n1, m1 = _mlstm_step(jnp.swapaxes(mqT[0], 0, 1), mk[0], jnp.swapaxes(mvT[0], 0, 1), gates[0],
                                     state_mlstm_C[i], state_mlstm_n[i], state_mlstm_m[i])
        hs = _channel_mixer(yconv[0], att.reshape(DB, -1), hm.reshape(DB, -1), og[0], hs[0], p_sample.reshape(depth, DB, -1),
                            lw_s, alpha, SAMPLE_EXPERT_BLOCK).reshape(1, DB, D)
        conv_new = jnp.stack([state_conv[i][:, 1, :], u[0]], axis=1)
        st_s.append((ckv.reshape(DB, 1, -1), krope.reshape(DB, 1, -1), conv_new, C1, n1, m1.reshape(DB, MLSTM_HEADS)))
    ckv_p, krope_p, conv_p, C_p, n_p, m_p = [jnp.stack(a) for a in zip(*st_p)]
    ckv_s, krope_s, conv_s, C_s, n_s, m_s = [jnp.stack(a) for a in zip(*st_s)]
    return (hp, hs.reshape(DB, DS, D), ckv_p, krope_p, conv_p, C_p, n_p, m_p,
            ckv_s, krope_s, conv_s, C_s, n_s, m_s)
```

```python
import functools

import jax
import jax.numpy as jnp
from jax import lax
from jax.experimental import pallas as pl
from jax.experimental.pallas import tpu as pltpu

F32 = jnp.float32
BF16 = jnp.bfloat16

LANES = 128
HEAD_DIM = 64
CONV_WIDTH = 256
CONV_K = 3
MLA_HEADS = 8
Q_RANK = 256
KV_RANK = 128
QK_NOPE = 64
QK_ROPE = 32
V_DIM = 64
ROPE_HALF = QK_ROPE // 2
ROPE_THETA = 10000.0
MLSTM_HEADS = 4
MLSTM_DH = 64
MLSTM_WIDTH = MLSTM_HEADS * MLSTM_DH
N_GROUPS = 4
EXPERTS_PER_GROUP = 8
N_EXPERTS = N_GROUPS * EXPERTS_PER_GROUP
TOP_K = 2
D_EXPERT = 256
LN_EPS = 1e-5
RMS_EPS = 1e-6
QK_SCALE = (QK_NOPE + QK_ROPE) ** -0.5
LOG2_E = 1.4426950408889634
Q_FOLD = QK_SCALE * LOG2_E

Z_WIDTH = 3 * CONV_WIDTH + Q_RANK + KV_RANK + 4 * MLSTM_WIDTH + LANES
GATE_LANE = QK_ROPE
ROUTER_LANE = N_GROUPS

VMEM_LIMIT = 56 * 1024 * 1024

TOKEN_TILE = 512
ATTN_TILE = 512
MLSTM_CHUNK = 256
COMBINE_TILE = 256
PROMPT_EXPERT_BLOCK = 256
SAMPLE_EXPERT_BLOCK = 16
PAGES_PER_CHUNK = 64
STEP_SEQS = 1


def _cparams(n_grid, row_dma=False):
    return pltpu.CompilerParams(dimension_semantics=("arbitrary",) * n_grid, vmem_limit_bytes=VMEM_LIMIT,
                                disable_bounds_checks=row_dma)


def _layer_norm(x, g, b):
    mu = jnp.mean(x, -1, keepdims=True)
    xc = x - mu
    var = jnp.mean(xc * xc, -1, keepdims=True)
    return xc * lax.rsqrt(var + LN_EPS) * g + b


def _rms_norm(x, g):
    return x * lax.rsqrt(jnp.mean(x * x, -1, keepdims=True) + RMS_EPS) * g


def _log_sigmoid(x):
    return jnp.minimum(x, 0.0) - jnp.log1p(jnp.exp(-jnp.abs(x)))


def _sigmoid(x):
    return 1.0 / (1.0 + jnp.exp(-x))


def _mm(a, w):
    if w.dtype == BF16:
        return jnp.dot(a.astype(BF16), w, preferred_element_type=F32)
    return jnp.dot(a.astype(F32), w, precision=lax.Precision.HIGHEST, preferred_element_type=F32)


def _rope_block(x, cos_t, sin_t, first_lane):
    lane = lax.broadcasted_iota(jnp.int32, x.shape, 1)
    in_first = (lane >= first_lane) & (lane < first_lane + ROPE_HALF)
    partner = jnp.where(in_first, pltpu.roll(x, LANES - ROPE_HALF, 1), pltpu.roll(x, ROPE_HALF, 1))
    return x * cos_t + partner * sin_t


def _in_proj_kernel(*refs, apply_ln0, seq_mode, emit_kv, tm):
    it = iter(refs)
    x_ref = next(it)
    if apply_ln0:
        g0_ref, b0_ref = next(it), next(it)
    w_in_ref, convw_ref = next(it), next(it)
    if not seq_mode:
        hist0_ref, hist1_ref = next(it), next(it)
    qg_ref, wuq_ref, kvg_ref = next(it), next(it), next(it)
    if emit_kv:
        wuk_ref, wuv_ref = next(it), next(it)
    gb_ref, cq_ref, sq_ref, ck_ref, sk_ref = next(it), next(it), next(it), next(it), next(it)
    if apply_ln0:
        xn_ref = next(it)
    yconv_ref, q_ref, ckv_ref, krope_ref = next(it), next(it), next(it), next(it)
    mqT_ref, mk_ref, mvT_ref, og_ref, gates_ref, gatesT_ref = (next(it) for _ in range(6))
    if emit_kv:
        k_ref, v_ref = next(it), next(it)
    if seq_mode:
        cstate_ref = next(it)
        ubuf = next(it)
    else:
        u_ref = next(it)

    lane = lax.broadcasted_iota(jnp.int32, (tm, LANES), 1)
    x = x_ref[...]
    if apply_ln0:
        x = _layer_norm(x, g0_ref[...], b0_ref[...])
        xn_ref[...] = x
    z = _mm(x, w_in_ref[...])
    o = 0
    cb = z[:, o:o + CONV_WIDTH]; o += CONV_WIDTH
    cc = z[:, o:o + CONV_WIDTH]; o += CONV_WIDTH
    ch = z[:, o:o + CONV_WIDTH]; o += CONV_WIDTH
    c_q = z[:, o:o + Q_RANK]; o += Q_RANK
    c_kv = z[:, o:o + KV_RANK]; o += KV_RANK
    mq = z[:, o:o + MLSTM_WIDTH]; o += MLSTM_WIDTH
    mk = z[:, o:o + MLSTM_WIDTH]; o += MLSTM_WIDTH
    mv = z[:, o:o + MLSTM_WIDTH]; o += MLSTM_WIDTH
    mo = z[:, o:o + MLSTM_WIDTH]; o += MLSTM_WIDTH
    last = z[:, o:o + LANES]

    u = cc * ch
    w = convw_ref[...]
    if seq_mode:
        s_idx = pl.program_id(1)

        @pl.when(s_idx == 0)
        def _():
            ubuf[0:8, :] = jnp.zeros((8, CONV_WIDTH), F32)

        ubuf[8:8 + tm, :] = u
        conv = ubuf[6:6 + tm, :] * w[0:1] + ubuf[7:7 + tm, :] * w[1:2] + u * w[2:3]
        ubuf[0:8, :] = ubuf[tm:tm + 8, :]

        @pl.when(s_idx == pl.num_programs(1) - 1)
        def _():
            cstate_ref[...] = ubuf[6:8, :]
    else:
        conv = hist0_ref[...] * w[0:1] + hist1_ref[...] * w[1:2] + u * w[2:3]
        u_ref[...] = u
    yconv_ref[...] = cb * conv

    cqn = _rms_norm(c_q, qg_ref[...])
    q = _mm(cqn, wuq_ref[...])
    cq, sq = cq_ref[...], sq_ref[...]
    for h in range(MLA_HEADS):
        qh = _rope_block(q[:, h * LANES:(h + 1) * LANES], cq, sq, QK_NOPE) * Q_FOLD
        q_ref[:, h * LANES:(h + 1) * LANES] = qh.astype(q_ref.dtype)

    ckv = _rms_norm(c_kv, kvg_ref[...])
    ckv_ref[...] = ckv
    kr = _rope_block(last, ck_ref[...], sk_ref[...], 0)
    krope_ref[...] = kr[:, 0:QK_ROPE]
    if emit_kv:
        ckv_b = ckv.astype(BF16)
        kfull = jnp.dot(ckv_b, wuk_ref[...], preferred_element_type=F32)
        kr_shift = pltpu.roll(kr, QK_NOPE, 1)
        for h in range(MLA_HEADS):
            k_ref[:, h * LANES:(h + 1) * LANES] = (kfull[:, h * LANES:(h + 1) * LANES] + kr_shift).astype(BF16)
        vfull = jnp.dot(ckv_b, wuv_ref[...], preferred_element_type=F32)
        ones_hi = jnp.where(lane >= V_DIM, 1.0, 0.0)
        for h in range(MLA_HEADS):
            v_ref[:, h * LANES:(h + 1) * LANES] = (vfull[:, h * LANES:(h + 1) * LANES] + ones_hi).astype(BF16)

    mqT_ref[...] = mq.T.astype(mqT_ref.dtype)
    mk_ref[...] = (mk * (MLSTM_DH ** -0.5)).astype(mk_ref.dtype)
    mvT_ref[...] = mv.T.astype(mvT_ref.dtype)
    og_ref[...] = mo
    g = last + gb_ref[...]
    is_forget = (lane >= GATE_LANE + MLSTM_HEADS) & (lane < GATE_LANE + 2 * MLSTM_HEADS)
    g = jnp.where(is_forget, _log_sigmoid(g), g)
    gates_ref[...] = g[:, GATE_LANE:GATE_LANE + 2 * MLSTM_HEADS]
    gatesT_ref[...] = g.T[GATE_LANE:GATE_LANE + 2 * MLSTM_HEADS, :]


def _in_proj(x, lw, tabs, *, apply_ln0, ln0, seq_mode, emit_kv, hist=None):
    B, S, D = x.shape
    tm = min(TOKEN_TILE, S)
    assert S % tm == 0
    ns = S // tm
    tok = lambda w: pl.BlockSpec((None, tm, w), lambda b, s: (b, s, 0))
    full2 = lambda a: pl.BlockSpec(a.shape, lambda b, s: (0, 0))
    tab = pl.BlockSpec((tm, LANES), lambda b, s: (s, 0))

    ins, specs = [x], [tok(D)]
    if apply_ln0:
        ins += [ln0[0], ln0[1]]; specs += [full2(ln0[0]), full2(ln0[1])]
    ins += [lw["w_in"], lw["conv_w"]]; specs += [full2(lw["w_in"]), full2(lw["conv_w"])]
    if not seq_mode:
        ins += [hist[0], hist[1]]; specs += [tok(CONV_WIDTH), tok(CONV_WIDTH)]
    ins += [lw["q_norm_g"], lw["w_uq"], lw["kv_norm_g"]]
    specs += [full2(lw["q_norm_g"]), full2(lw["w_uq"]), full2(lw["kv_norm_g"])]
    if emit_kv:
        ins += [lw["w_uk"], lw["w_uv"]]; specs += [full2(lw["w_uk"]), full2(lw["w_uv"])]
    ins += [lw["gate_b"]] + list(tabs)
    specs += [full2(lw["gate_b"])] + [tab] * 4

    outs, ospecs = [], []

    def add(shape, dtype, spec):
        outs.append(jax.ShapeDtypeStruct(shape, dtype)); ospecs.append(spec)

    if apply_ln0:
        add((B, S, D), F32, tok(D))
    add((B, S, CONV_WIDTH), F32, tok(CONV_WIDTH))
    add((B, S, MLA_HEADS * LANES), BF16 if emit_kv else F32, tok(MLA_HEADS * LANES))
    add((B, S, KV_RANK), F32, tok(KV_RANK))
    add((B, S, QK_ROPE), F32, tok(QK_ROPE))
    mdt = BF16 if seq_mode else F32
    tokT = pl.BlockSpec((None, MLSTM_WIDTH, tm), lambda b, s: (b, 0, s))
    add((B, MLSTM_WIDTH, S), mdt, tokT)
    add((B, S, MLSTM_WIDTH), mdt, tok(MLSTM_WIDTH))
    add((B, MLSTM_WIDTH, S), mdt, tokT)
    add((B, S, MLSTM_WIDTH), F32, tok(MLSTM_WIDTH))
    add((B, S, 2 * MLSTM_HEADS), F32, tok(2 * MLSTM_HEADS))
    add((B, 2 * MLSTM_HEADS, S), F32, pl.BlockSpec((None, 2 * MLSTM_HEADS, tm), lambda b, s: (b, 0, s)))
    if emit_kv:
        add((B, S, MLA_HEADS * LANES), BF16, tok(MLA_HEADS * LANES))
        add((B, S, MLA_HEADS * LANES), BF16, tok(MLA_HEADS * LANES))
    scratch = []
    if seq_mode:
        add((B, CONV_K - 1, CONV_WIDTH), F32, pl.BlockSpec((None, CONV_K - 1, CONV_WIDTH), lambda b, s: (b, 0, 0)))
        scratch = [pltpu.VMEM((tm + 8, CONV_WIDTH), F32)]
    else:
        add((B, S, CONV_WIDTH), F32, tok(CONV_WIDTH))
    kern = functools.partial(_in_proj_kernel, apply_ln0=apply_ln0, seq_mode=seq_mode, emit_kv=emit_kv, tm=tm)
    return pl.pallas_call(kern, grid=(B, ns), in_specs=specs, out_specs=ospecs, out_shape=outs,
                          scratch_shapes=scratch, compiler_params=_cparams(2), name="in_proj")(*ins)


def _flash_attn_kernel(q_ref, k_ref, v_ref, o_ref, *, tq):
    qi = pl.program_id(2)
    row = lax.broadcasted_iota(jnp.int32, (tq, tq), 0)
    col = lax.broadcasted_iota(jnp.int32, (tq, tq), 1)
    causal = col <= row

    def block(ki, carry, masked):
        start = pl.multiple_of(ki * tq, tq)
        new = []
        for hh in range(2):
            m, acc = carry[hh]
            hl = slice(hh * LANES, (hh + 1) * LANES)
            k = k_ref[pl.ds(start, tq), hl]
            v = v_ref[pl.ds(start, tq), hl]
            s = lax.dot_general(q_ref[:, hl], k, (((1,), (1,)), ((), ())), preferred_element_type=F32)
            if masked:
                s = jnp.where(causal, s, -jnp.inf)
            m_new = jnp.maximum(m, jnp.max(s, -1, keepdims=True))
            p = jnp.exp2(s - m_new)
            acc = jnp.exp2(m - m_new) * acc + jnp.dot(p.astype(BF16), v, preferred_element_type=F32)
            new.append((m_new, acc))
        return tuple(new)

    init = ((jnp.full((tq, 1), -jnp.inf, F32), jnp.zeros((tq, LANES), F32)),) * 2

    def two_blocks(j, carry):
        return block(2 * j + 1, block(2 * j, carry, masked=False), masked=False)

    carry = lax.fori_loop(0, qi // 2, two_blocks, init)
    carry = lax.cond(qi % 2 == 1, lambda c: block(qi - 1, c, masked=False), lambda c: c, carry)
    fin = block(qi, carry, masked=True)
    outs = [acc / pltpu.roll(acc, V_DIM, 1) for _, acc in fin]
    lane = lax.broadcasted_iota(jnp.int32, (tq, LANES), 1)
    o_ref[...] = jnp.where(lane < V_DIM, outs[0], pltpu.roll(outs[1], V_DIM, 1))


def _flash_attn(q, k, v):
    B, S, _ = q.shape
    tq = min(ATTN_TILE, S)
    assert S % tq == 0
    kern = functools.partial(_flash_attn_kernel, tq=tq)
    return pl.pallas_call(
        kern, grid=(B, MLA_HEADS // 2, S // tq),
        in_specs=[pl.BlockSpec((None, tq, 2 * LANES), lambda b, h, i: (b, i, h)),
                  pl.BlockSpec((None, S, 2 * LANES), lambda b, h, i: (b, 0, h)),
                  pl.BlockSpec((None, S, 2 * LANES), lambda b, h, i: (b, 0, h))],
        out_specs=pl.BlockSpec((None, tq, 2 * V_DIM), lambda b, h, i: (b, i, h)),
        out_shape=jax.ShapeDtypeStruct((B, S, MLA_HEADS * V_DIM), F32),
        compiler_params=_cparams(3), name="flash_attn")(q, k, v)


def _paged_attn_kernel(pt_ref, q_ref, ckvn_ref, krn_ref, wukT_ref, wuv_ref, cckv_ref, ckrT_ref, o_ref,
                       kbuf, rbuf, sems, qlat_scr, qr_scr, m_scr, l_scr, acc_scr, *, layer, npc, nchunks, page):
    b, c = pl.program_id(0), pl.program_id(1)
    g = b * nchunks + c
    total = pl.num_programs(0) * nchunks
    slot = g % 2

    def page_copies(bb, cc, sl, j):
        pg = pt_ref[bb, cc * npc + j]
        return (pltpu.make_async_copy(cckv_ref.at[layer, pg], kbuf.at[sl, j], sems.at[0, sl]),
                pltpu.make_async_copy(ckrT_ref.at[layer, pg], rbuf.at[sl, j], sems.at[1, sl]))

    def start_chunk(bb, cc, sl):
        def issue(j, _):
            for cp in page_copies(bb, cc, sl, j):
                cp.start()
            return 0
        lax.fori_loop(0, npc, issue, 0, unroll=4)

    @pl.when(g == 0)
    def _():
        start_chunk(b, c, slot)

    @pl.when(g + 1 < total)
    def _():
        nxt = g + 1
        start_chunk(nxt // nchunks, nxt % nchunks, 1 - slot)

    @pl.when(c == 0)
    def _():
        qrow = q_ref[0]
        for h in range(MLA_HEADS):
            qn = jnp.broadcast_to(qrow[:, h * LANES:h * LANES + QK_NOPE], (8, QK_NOPE)).astype(BF16)
            qlat_scr[h:h + 1, :] = jnp.dot(qn, wukT_ref[h], preferred_element_type=F32)[0:1]
            qr_scr[h:h + 1, :] = qrow[:, h * LANES + QK_NOPE:h * LANES + QK_NOPE + QK_ROPE]
        m_scr[...] = jnp.full(m_scr.shape, -jnp.inf, F32)
        l_scr[...] = jnp.zeros(l_scr.shape, F32)
        acc_scr[...] = jnp.zeros(acc_scr.shape, F32)

    pltpu.make_async_copy(cckv_ref.at[layer, pl.ds(0, npc)], kbuf.at[slot], sems.at[0, slot]).wait()
    pltpu.make_async_copy(ckrT_ref.at[layer, pl.ds(0, npc)], rbuf.at[slot], sems.at[1, slot]).wait()

    kb = kbuf[slot].reshape(npc * page, KV_RANK).astype(BF16)
    qr = qr_scr[...].astype(BF16)
    s_rope = jnp.concatenate([jnp.dot(qr, rbuf[slot, j].astype(BF16), preferred_element_type=F32)
                              for j in range(npc)], axis=-1)
    s = lax.dot_general(qlat_scr[...].astype(BF16), kb, (((1,), (1,)), ((), ())),
                        preferred_element_type=F32) + s_rope
    m = m_scr[...]
    m_new = jnp.maximum(m, jnp.max(s, -1, keepdims=True))
    alpha = jnp.exp2(m - m_new)
    p = jnp.exp2(s - m_new)
    l_scr[...] = alpha * l_scr[...] + jnp.sum(p, -1, keepdims=True)
    acc_scr[...] = alpha * acc_scr[...] + jnp.dot(p.astype(BF16), kb, preferred_element_type=F32)
    m_scr[...] = m_new

    @pl.when(c == nchunks - 1)
    def _():
        ckv_new, kr_new = ckvn_ref[0], krn_ref[0]
        s_new = (jnp.sum(qlat_scr[...] * ckv_new, -1, keepdims=True)
                 + jnp.sum(qr_scr[...] * kr_new, -1, keepdims=True))
        m_old = m_scr[...]
        m_fin = jnp.maximum(m_old, s_new)
        a = jnp.exp2(m_old - m_fin)
        p_new = jnp.exp2(s_new - m_fin)
        l_fin = a * l_scr[...] + p_new
        o_lat = (a * acc_scr[...] + p_new * ckv_new) / l_fin
        for h in range(MLA_HEADS):
            oh = jnp.broadcast_to(o_lat[h:h + 1, :], (8, KV_RANK)).astype(BF16)
            o_ref[0, :, h * V_DIM:(h + 1) * V_DIM] = jnp.dot(oh, wuv_ref[h], preferred_element_type=F32)[0:1]


def _paged_attn(q, ckv_new, kr_new, wukT, wuv_h, cache_ckv, cache_krope_t, page_table, layer):
    DB = q.shape[0]
    n_pages = page_table.shape[1]
    page = cache_ckv.shape[2]
    npc = min(PAGES_PER_CHUNK, n_pages)
    assert n_pages % npc == 0
    nchunks = n_pages // npc
    kern = functools.partial(_paged_attn_kernel, layer=layer, npc=npc, nchunks=nchunks, page=page)
    row = lambda w: pl.BlockSpec((1, 1, w), lambda b, c, pt: (b, 0, 0))
    full3 = lambda a: pl.BlockSpec(a.shape, lambda b, c, pt: (0, 0, 0))
    gs = pltpu.PrefetchScalarGridSpec(
        num_scalar_prefetch=1, grid=(DB, nchunks),
        in_specs=[row(MLA_HEADS * LANES), row(KV_RANK), row(QK_ROPE), full3(wukT), full3(wuv_h),
                  pl.BlockSpec(memory_space=pl.ANY), pl.BlockSpec(memory_space=pl.ANY)],
        out_specs=row(MLA_HEADS * V_DIM),
        scratch_shapes=[pltpu.VMEM((2, npc, page, KV_RANK), F32), pltpu.VMEM((2, npc, QK_ROPE, page), F32),
                        pltpu.SemaphoreType.DMA((2, 2)),
                        pltpu.VMEM((MLA_HEADS, KV_RANK), F32), pltpu.VMEM((MLA_HEADS, QK_ROPE), F32),
                        pltpu.VMEM((MLA_HEADS, 1), F32), pltpu.VMEM((MLA_HEADS, 1), F32),
                        pltpu.VMEM((MLA_HEADS, KV_RANK), F32)])
    return pl.pallas_call(kern, grid_spec=gs, out_shape=jax.ShapeDtypeStruct((DB, 1, MLA_HEADS * V_DIM), F32),
                          compiler_params=_cparams(2, row_dma=True), name="paged_attn")(
        page_table, q, ckv_new, kr_new, wukT, wuv_h, cache_ckv, cache_krope_t)


def _mlstm_seq_kernel(qT_ref, k_ref, vT_ref, g_ref, gT_ref, h_ref, c_out, n_out, m_out, c_scr, n_scr, m_scr, hT_scr, *, L):
    c = pl.program_id(1)

    @pl.when(c == 0)
    def _():
        c_scr[...] = jnp.zeros(c_scr.shape, F32)
        n_scr[...] = jnp.zeros(n_scr.shape, F32)
        m_scr[...] = jnp.zeros(m_scr.shape, F32)

    g = g_ref[...]
    gT = gT_ref[...]
    src = lax.broadcasted_iota(jnp.int32, (L, L), 0)
    tgt = lax.broadcasted_iota(jnp.int32, (L, L), 1)
    causal = src <= tgt
    b_cols = jnp.dot((tgt <= src).astype(F32), g, precision=lax.Precision.HIGHEST, preferred_element_type=F32)
    b_rows = jnp.dot(gT, causal.astype(F32), precision=lax.Precision.HIGHEST, preferred_element_type=F32)
    for h in range(MLSTM_HEADS):
        sl = slice(h * MLSTM_DH, (h + 1) * MLSTM_DH)
        qT, kh, vT = qT_ref[sl, :], k_ref[:, sl], vT_ref[sl, :]
        a_col = g[:, h:h + 1] - b_cols[:, MLSTM_HEADS + h:MLSTM_HEADS + h + 1]
        b_row = b_rows[MLSTM_HEADS + h:MLSTM_HEADS + h + 1, :]
        i_row = gT[h:h + 1, :]
        m_prev = m_scr[h:h + 1, 0:1]
        log_d = jnp.where(causal, a_col + b_row, -jnp.inf)
        m_inter = b_row + m_prev
        m_t = jnp.maximum(m_inter, jnp.max(log_d, 0, keepdims=True))
        w_inter = jnp.exp(m_inter - m_t)
        sT = jnp.dot(kh, qT, preferred_element_type=F32) * jnp.exp(log_d - m_t)
        C = c_scr[h]
        n8 = jnp.broadcast_to(n_scr[h:h + 1, :], (8, MLSTM_DH)).astype(BF16)
        numT = (jnp.dot(vT, sT.astype(BF16), preferred_element_type=F32)
                + w_inter * jnp.dot(C.astype(BF16), qT, preferred_element_type=F32))
        den = jnp.sum(sT, 0, keepdims=True) + w_inter * jnp.dot(n8, qT, preferred_element_type=F32)[0:1]
        hT_scr[sl, :] = numT / jnp.maximum(jnp.abs(den), jnp.exp(-m_t))
        m_new = m_t[:, L - 1:L]
        b_last = b_row[:, L - 1:L]
        w_old = jnp.exp(b_last + m_prev - m_new)
        w_s = jnp.exp(b_last - b_row + i_row - m_new)
        c_scr[h] = w_old * C + jnp.dot((vT.astype(F32) * w_s).astype(BF16), kh, preferred_element_type=F32)
        ws8 = jnp.broadcast_to(w_s, (8, L)).astype(BF16)
        n_scr[h:h + 1, :] = w_old * n_scr[h:h + 1, :] + jnp.dot(ws8, kh, preferred_element_type=F32)[0:1]
        m_scr[h:h + 1, :] = jnp.broadcast_to(m_new, (1, LANES))
    h_ref[...] = hT_scr[...].T

    @pl.when(c == pl.num_programs(1) - 1)
    def _():
        c_out[...] = c_scr[...]
        n_out[...] = n_scr[...]
        m_out[...] = m_scr[...]


def _eye(n):
    return (lax.broadcasted_iota(jnp.int32, (n, n), 0) == lax.broadcasted_iota(jnp.int32, (n, n), 1)).astype(F32)


def _col_to_row(col):
    return jnp.sum(_eye(col.shape[0]) * col, 0, keepdims=True)


def _row_to_col(row):
    return jnp.sum(_eye(row.shape[1]) * row, 1, keepdims=True)


def _mlstm_seq(mqT, mk, mvT, gates, gatesT):
    B, S, _ = mk.shape
    L = min(MLSTM_CHUNK, S)
    assert S % L == 0
    kern = functools.partial(_mlstm_seq_kernel, L=L)
    tok = lambda w: pl.BlockSpec((None, L, w), lambda b, c: (b, c, 0))
    tokT = lambda w: pl.BlockSpec((None, w, L), lambda b, c: (b, 0, c))
    st = lambda *dims: pl.BlockSpec((None,) + dims, lambda b, c: (b,) + (0,) * len(dims))
    return pl.pallas_call(
        kern, grid=(B, S // L),
        in_specs=[tokT(MLSTM_WIDTH), tok(MLSTM_WIDTH), tokT(MLSTM_WIDTH), tok(2 * MLSTM_HEADS), tokT(2 * MLSTM_HEADS)],
        out_specs=[tok(MLSTM_WIDTH), st(MLSTM_HEADS, MLSTM_DH, MLSTM_DH), st(MLSTM_HEADS, MLSTM_DH), st(8, LANES)],
        out_shape=[jax.ShapeDtypeStruct((B, S, MLSTM_WIDTH), F32),
                   jax.ShapeDtypeStruct((B, MLSTM_HEADS, MLSTM_DH, MLSTM_DH), F32),
                   jax.ShapeDtypeStruct((B, MLSTM_HEADS, MLSTM_DH), F32),
                   jax.ShapeDtypeStruct((B, 8, LANES), F32)],
        scratch_shapes=[pltpu.VMEM((MLSTM_HEADS, MLSTM_DH, MLSTM_DH), F32), pltpu.VMEM((MLSTM_HEADS, MLSTM_DH), F32),
                        pltpu.VMEM((8, LANES), F32), pltpu.VMEM((MLSTM_WIDTH, L), F32)],
        compiler_params=_cparams(2), name="mlstm_seq")(mqT, mk, mvT, gates, gatesT)


def _mlstm_step_kernel(q_ref, k_ref, v_ref, g_ref, c_ref, n_ref, m_ref, h_ref, c_out, n_out, m_out):
    for i in range(q_ref.shape[0]):
        g = g_ref[i]
        m_all = m_ref[i]
        for h in range(MLSTM_HEADS):
            sl = slice(h * MLSTM_DH, (h + 1) * MLSTM_DH)
            q, k, v = q_ref[i][:, sl], k_ref[i][:, sl], v_ref[i][:, sl]
            C, n = c_ref[i, h], n_ref[i, h:h + 1, :]
            ig, lf = g[:, h:h + 1], g[:, MLSTM_HEADS + h:MLSTM_HEADS + h + 1]
            m_prev = m_all[:, h:h + 1]
            m_inter = lf + m_prev
            m_t = jnp.maximum(m_inter, ig)
            d = jnp.exp(ig - m_t)
            w_inter = jnp.exp(m_inter - m_t)
            s = jnp.sum(q * k, -1, keepdims=True) * d
            cq = jnp.sum(C * q, -1, keepdims=True)
            v_col = _row_to_col(v)
            num = s * v_col + w_inter * cq
            den = s + w_inter * jnp.sum(n * q, -1, keepdims=True)
            h_col = num / jnp.maximum(jnp.abs(den), jnp.exp(-m_t))
            h_ref[i, :, sl] = _col_to_row(h_col)
            c_out[i, h] = w_inter * C + d * (v_col * k)
            n_out[i, h:h + 1, :] = w_inter * n + d * k
            m_out[i, :, h:h + 1] = m_t


def _mlstm_step(mq, mk, mv, gates, C0, n0, m0):
    DB = mq.shape[0]
    sb = STEP_SEQS if DB % STEP_SEQS == 0 else 1
    r3 = lambda a: a.reshape(DB, 1, a.shape[-1])
    row = lambda w: pl.BlockSpec((sb, 1, w), lambda b: (b, 0, 0))
    cspec = pl.BlockSpec((sb, MLSTM_HEADS, MLSTM_DH, MLSTM_DH), lambda b: (b, 0, 0, 0))
    nspec = pl.BlockSpec((sb, MLSTM_HEADS, MLSTM_DH), lambda b: (b, 0, 0))
    return pl.pallas_call(
        _mlstm_step_kernel, grid=(DB // sb,),
        in_specs=[row(MLSTM_WIDTH)] * 3 + [row(2 * MLSTM_HEADS), cspec, nspec, row(MLSTM_HEADS)],
        out_specs=[row(MLSTM_WIDTH), cspec, nspec, row(MLSTM_HEADS)],
        out_shape=[jax.ShapeDtypeStruct((DB, 1, MLSTM_WIDTH), F32), jax.ShapeDtypeStruct(C0.shape, F32),
                   jax.ShapeDtypeStruct(n0.shape, F32), jax.ShapeDtypeStruct((DB, 1, MLSTM_HEADS), F32)],
        compiler_params=_cparams(1), name="mlstm_step")(r3(mq), r3(mk), r3(mv), r3(gates), C0, n0, r3(m0))


def _mix_out_kernel(yc_ref, att_ref, hm_ref, og_ref, x_ref, mg_ref, wout_ref, g1_ref, b1_ref, wr_ref, br_ref,
                    h1_ref, route_ref, cnt_ref, ymix_scr, carry_scr, *, tm, alpha):
    @pl.when(pl.program_id(0) == 0)
    def _():
        carry_scr[...] = jnp.zeros(carry_scr.shape, F32)

    lane = lax.broadcasted_iota(jnp.int32, (tm, LANES), 1)
    low = lane < HEAD_DIM

    def norm_store(y, j):
        y2 = y * y
        s_lo = jnp.sum(jnp.where(low, y2, 0.0), -1, keepdims=True)
        s_hi = jnp.sum(jnp.where(low, 0.0, y2), -1, keepdims=True)
        inv = jnp.where(low, lax.rsqrt(s_lo / HEAD_DIM + RMS_EPS), lax.rsqrt(s_hi / HEAD_DIM + RMS_EPS))
        ymix_scr[:, j * LANES:(j + 1) * LANES] = (y * inv * mg_ref[:, j * LANES:(j + 1) * LANES]).astype(ymix_scr.dtype)

    j = 0
    for src, width in ((yc_ref, CONV_WIDTH), (att_ref, MLA_HEADS * V_DIM)):
        for t in range(width // LANES):
            norm_store(src[:, t * LANES:(t + 1) * LANES], j)
            j += 1
    for t in range(MLSTM_WIDTH // LANES):
        sl = slice(t * LANES, (t + 1) * LANES)
        norm_store(_sigmoid(og_ref[:, sl]) * hm_ref[:, sl], j)
        j += 1

    proj = _mm(ymix_scr[...], wout_ref[...])
    h1 = _layer_norm(alpha * x_ref[...] + proj, g1_ref[...], b1_ref[...])
    h1_ref[...] = h1

    h_hi = h1.astype(BF16)
    h_lo = (h1 - h_hi.astype(F32)).astype(BF16)
    part = jnp.dot(h_hi, wr_ref[...], preferred_element_type=F32)
    logits = (part[:, 0:LANES] + part[:, LANES:2 * LANES]
              + jnp.dot(h_lo, wr_ref[:, 0:LANES], preferred_element_type=F32) + br_ref[...])
    lanef = lane.astype(F32)
    big = float(LANES)
    neg = -jnp.inf
    is_g = lane < N_GROUPS
    gl = jnp.where(is_g, logits, neg)
    gmax = jnp.max(gl, -1, keepdims=True)
    g_sel = jnp.min(jnp.where(gl == gmax, lanef, big), -1, keepdims=True)
    g_w = 1.0 / jnp.sum(jnp.where(is_g, jnp.exp(logits - gmax), 0.0), -1, keepdims=True)
    lo = ROUTER_LANE + EXPERTS_PER_GROUP * g_sel
    in_grp = (lanef >= lo) & (lanef < lo + EXPERTS_PER_GROUP)
    el = jnp.where(in_grp, logits, neg)
    e1 = jnp.max(el, -1, keepdims=True)
    i1 = jnp.min(jnp.where(el == e1, lanef, big), -1, keepdims=True)
    el2 = jnp.where(lanef == i1, neg, el)
    e2 = jnp.max(el2, -1, keepdims=True)
    i2 = jnp.min(jnp.where(el2 == e2, lanef, big), -1, keepdims=True)
    zsum = jnp.sum(jnp.where(in_grp, jnp.exp(logits - e1), 0.0), -1, keepdims=True)
    p1 = 1.0 / zsum
    p2 = jnp.exp(e2 - e1) / zsum
    psum = p1 + p2
    w1 = p1 / psum * g_w
    w2 = p2 / psum * g_w

    oh1 = (lanef == i1).astype(F32)
    oh2 = (lanef == i2).astype(F32)
    both = oh1 + oh2
    r = lax.broadcasted_iota(jnp.int32, (tm, tm), 0)
    c = lax.broadcasted_iota(jnp.int32, (tm, tm), 1)
    before = jnp.dot((c < r).astype(BF16), both.astype(BF16), preferred_element_type=F32) + carry_scr[0:1, :]
    r1 = jnp.sum(oh1 * before, -1, keepdims=True)
    r2 = jnp.sum(oh2 * before, -1, keepdims=True)
    carry = carry_scr[0:1, :] + jnp.sum(both, 0, keepdims=True)
    carry_scr[...] = jnp.broadcast_to(carry, carry_scr.shape)
    cnt_ref[...] = jnp.broadcast_to(carry, cnt_ref.shape)

    vals = (i1 - ROUTER_LANE, i2 - ROUTER_LANE, r1, r2, w1, w2)
    packed = jnp.zeros((tm, LANES), F32)
    for idx, val in enumerate(vals):
        packed = jnp.where(lane == idx, val, packed)
    route_ref[...] = packed[:, 0:8]


def _mix_out(yconv, att, hm, og, x, lw, alpha):
    T, D = x.shape
    tm = min(TOKEN_TILE, T)
    assert T % tm == 0
    tok = lambda w: pl.BlockSpec((tm, w), lambda i: (i, 0))
    full = lambda a: pl.BlockSpec(a.shape, lambda i: (0, 0))
    ws = [lw["mix_norm_g"], lw["w_out"], lw["ln1_g"], lw["ln1_b"], lw["w_router"], lw["b_router"]]
    kern = functools.partial(_mix_out_kernel, tm=tm, alpha=alpha)
    return pl.pallas_call(
        kern, grid=(T // tm,),
        in_specs=[tok(CONV_WIDTH), tok(MLA_HEADS * V_DIM), tok(MLSTM_WIDTH), tok(MLSTM_WIDTH), tok(D)] + [full(a) for a in ws],
        out_specs=[tok(D), tok(8), pl.BlockSpec((8, LANES), lambda i: (0, 0))],
        out_shape=[jax.ShapeDtypeStruct((T, D), F32), jax.ShapeDtypeStruct((T, 8), F32),
                   jax.ShapeDtypeStruct((8, LANES), F32)],
        scratch_shapes=[pltpu.VMEM((tm, D), lw["w_out"].dtype), pltpu.VMEM((8, LANES), F32)],
        compiler_params=_cparams(1), name="mix_out")(yconv, att, hm, og, x, *ws)


def _ple_dispatch_kernel(pend_ref, dest_ref, h1_ref, p_ref, wpe_ref, wpg_ref, ple_ref, xs_out, sem, zsem, zbuf,
                         *, tm, blk):
    @pl.when(pl.program_id(0) == 0)
    def _():
        zbuf[...] = jnp.zeros(zbuf.shape, F32)
        def for_nonempty_experts(action):
            for e in range(N_EXPERTS):
                end = pend_ref[e]
                first = pend_ref[e - 1] if e else 0
                copy = pltpu.make_async_copy(zbuf, xs_out.at[pl.ds(pl.multiple_of(end - blk, blk), blk)], zsem)
                pl.when(end > first)(functools.partial(action, copy))

        for_nonempty_experts(lambda copy: copy.start())
        for_nonempty_experts(lambda copy: copy.wait())

    for r in range(tm):
        for k in range(TOP_K):
            pltpu.make_async_copy(h1_ref.at[pl.ds(r, 1)], xs_out.at[pl.ds(dest_ref[TOP_K * r + k], 1)], sem).start()

    ple_ref[...] = _mm(p_ref[...], wpe_ref[...]) * _sigmoid(_mm(h1_ref[...], wpg_ref[...]))

    for k in range(TOP_K):
        pltpu.make_async_copy(h1_ref, xs_out.at[pl.ds(0, tm)], sem).wait()


def _ple_dispatch(pad_end, dest_flat, h1, p, lw, n_rows, blk):
    T, D = h1.shape
    tm = min(TOKEN_TILE, T)
    tok = lambda w: pl.BlockSpec((tm, w), lambda i, pe: (i, 0))
    full = lambda a: pl.BlockSpec(a.shape, lambda i, pe: (0, 0))
    kern = functools.partial(_ple_dispatch_kernel, tm=tm, blk=blk)
    gs = pltpu.PrefetchScalarGridSpec(
        num_scalar_prefetch=1, grid=(T // tm,),
        in_specs=[pl.BlockSpec((TOP_K * tm,), lambda i, pe: (i,), memory_space=pltpu.SMEM),
                  tok(D), pl.BlockSpec((None, tm, p.shape[2]), lambda i, pe: (lw["layer"], i, 0)),
                  full(lw["w_pe"]), full(lw["w_pg"])],
        out_specs=[tok(D), pl.BlockSpec(memory_space=pl.ANY)],
        scratch_shapes=[pltpu.SemaphoreType.DMA(()), pltpu.SemaphoreType.DMA(()), pltpu.VMEM((blk, D), F32)])
    ple, xs = pl.pallas_call(
        kern, grid_spec=gs,
        out_shape=[jax.ShapeDtypeStruct((T, D), F32), jax.ShapeDtypeStruct((n_rows, D), F32)],
        compiler_params=_cparams(1, row_dma=True), name="ple_dispatch")(pad_end, dest_flat, h1, p, lw["w_pe"], lw["w_pg"])
    return ple, xs


def _experts_kernel(be_ref, nu_ref, xs_ref, wg_ref, wu_ref, wd_ref, ys_ref):
    del be_ref
    i = pl.program_id(0)

    @pl.when(i < nu_ref[0])
    def _():
        x = xs_ref[...]
        gt, up = _mm(x, wg_ref[0]), _mm(x, wu_ref[0])
        ys_ref[...] = _mm(gt * _sigmoid(gt) * up, wd_ref[0])

    @pl.when(i >= nu_ref[0])
    def _():
        ys_ref[...] = jnp.zeros(ys_ref.shape, F32)


def _experts(blk_expert, n_used, xs, lw, blk):
    n_rows, D = xs.shape
    layer = lw["layer"]
    wspec = lambda a: pl.BlockSpec((None, 1) + a.shape[2:], lambda i, be, nu: (layer, be[i], 0, 0))
    gs = pltpu.PrefetchScalarGridSpec(
        num_scalar_prefetch=2, grid=(n_rows // blk,),
        in_specs=[pl.BlockSpec((blk, D), lambda i, be, nu: (jnp.minimum(i, nu[0] - 1), 0)),
                  wspec(lw["e_gate"]), wspec(lw["e_up"]), wspec(lw["e_down"])],
        out_specs=pl.BlockSpec((blk, D), lambda i, be, nu: (i, 0)))
    return pl.pallas_call(_experts_kernel, grid_spec=gs, out_shape=jax.ShapeDtypeStruct((n_rows, D), F32),
                          compiler_params=_cparams(1), name="experts")(
        blk_expert, n_used, xs, lw["e_gate"], lw["e_up"], lw["e_down"])


def _combine_kernel(dest_ref, dnext_ref, h1_ref, ple_ref, route_ref, g2_ref, b2_ref, ys_hbm, out_ref, gbuf, sems,
                    *, tm, alpha):
    i = pl.program_id(0)
    slot = i % 2

    def start_gather(d_ref, sl):
        def issue(r, _):
            for k in range(TOP_K):
                pltpu.make_async_copy(ys_hbm.at[pl.ds(d_ref[TOP_K * r + k], 1)], gbuf.at[sl, k, pl.ds(r, 1)],
                                      sems.at[sl]).start()
            return 0
        lax.fori_loop(0, tm, issue, 0, unroll=8)

    @pl.when(i == 0)
    def _():
        start_gather(dest_ref, slot)

    @pl.when(i + 1 < pl.num_programs(0))
    def _():
        start_gather(dnext_ref, 1 - slot)

    for k in range(TOP_K):
        pltpu.make_async_copy(ys_hbm.at[pl.ds(0, tm)], gbuf.at[slot, k], sems.at[slot]).wait()

    route = route_ref[...]
    moe = gbuf[slot, 0] * route[:, 4:5] + gbuf[slot, 1] * route[:, 5:6]
    out_ref[...] = _layer_norm(alpha * h1_ref[...] + moe + ple_ref[...], g2_ref[...], b2_ref[...])


def _combine(dest_flat, h1, ple, route, ys, lw, alpha):
    T, D = h1.shape
    tm = min(COMBINE_TILE, T)
    tok = lambda w: pl.BlockSpec((tm, w), lambda i: (i, 0))
    full = lambda a: pl.BlockSpec(a.shape, lambda i: (0, 0))
    kern = functools.partial(_combine_kernel, tm=tm, alpha=alpha)
    last = T // tm - 1
    return pl.pallas_call(
        kern, grid=(T // tm,),
        in_specs=[pl.BlockSpec((TOP_K * tm,), lambda i: (i,), memory_space=pltpu.SMEM),
                  pl.BlockSpec((TOP_K * tm,), lambda i: (jnp.minimum(i + 1, last),), memory_space=pltpu.SMEM),
                  tok(D), tok(D), tok(8), full(lw["ln2_g"]), full(lw["ln2_b"]), pl.BlockSpec(memory_space=pl.ANY)],
        out_specs=tok(D), out_shape=jax.ShapeDtypeStruct((T, D), F32),
        scratch_shapes=[pltpu.VMEM((2, TOP_K, tm, D), F32), pltpu.SemaphoreType.DMA((2,))],
        compiler_params=_cparams(1, row_dma=True), name="combine")(
        dest_flat, dest_flat, h1, ple, route, lw["ln2_g"], lw["ln2_b"], ys)


def _channel_mixer(yconv, att, hm, og, x, p, lw, alpha, blk):
    T, D = x.shape
    h1, route, cnt = _mix_out(yconv, att, hm, og, x, lw, alpha)
    counts = cnt[0, ROUTER_LANE:ROUTER_LANE + N_EXPERTS].astype(jnp.int32)
    padded = (counts + blk - 1) // blk * blk
    pad_end = jnp.cumsum(padded)
    pad_start = pad_end - padded
    n_blocks = -(-T * TOP_K // blk) + N_EXPERTS
    eid = route[:, 0:TOP_K].astype(jnp.int32)
    rank = route[:, TOP_K:2 * TOP_K].astype(jnp.int32)
    dest_flat = (pad_start[eid] + rank).reshape(T * TOP_K)
    blk_first_row = jnp.arange(n_blocks, dtype=jnp.int32) * blk
    blk_expert = jnp.minimum(jnp.sum((pad_end[None, :] <= blk_first_row[:, None]).astype(jnp.int32), -1), N_EXPERTS - 1)
    n_used = (pad_end[-1:] // blk).astype(jnp.int32)
    ple, xs = _ple_dispatch(pad_end.astype(jnp.int32), dest_flat, h1, p, lw, n_blocks * blk, blk)
    ys = _experts(blk_expert, n_used, xs, lw, blk)
    return _combine(dest_flat, h1, ple, route, ys, lw, alpha)


def _rope_tables(pos, n_rows):
    inv = ROPE_THETA ** (-jnp.arange(ROPE_HALF, dtype=F32) / ROPE_HALF)
    ang = pos.astype(F32)[:, None] * inv
    cos, sin = jnp.cos(ang), jnp.sin(ang)
    if cos.shape[0] != n_rows:
        cos, sin = jnp.broadcast_to(cos, (n_rows, ROPE_HALF)), jnp.broadcast_to(sin, (n_rows, ROPE_HALF))
    z = lambda w: jnp.zeros((n_rows, w), F32)
    one = jnp.ones((n_rows, QK_NOPE), F32)
    pad_q = LANES - QK_NOPE - QK_ROPE
    cq = jnp.concatenate([one, cos, cos, z(pad_q)], -1)
    sq = jnp.concatenate([z(QK_NOPE), -sin, sin, z(pad_q)], -1)
    ck = jnp.concatenate([cos, cos, z(LANES - QK_ROPE)], -1)
    sk = jnp.concatenate([-sin, sin, z(LANES - QK_ROPE)], -1)
    return cq, sq, ck, sk


def _prep_layer(i, w_in, conv_w, q_norm_g, w_uq, kv_norm_g, w_ukv, mlstm_gate_b, mix_norm_g, w_out, ln1_g, ln1_b,
                w_group, b_group, w_expert, b_expert, e_gate, e_up, e_down, w_pe, w_pg, ln2_g, ln2_b, e_bf16):
    D = w_in.shape[1]
    cuts, o = [], 0
    for wdt in [CONV_WIDTH] * 3 + [Q_RANK, KV_RANK, QK_ROPE] + [MLSTM_WIDTH] * 4 + [MLSTM_HEADS] * 2:
        cuts.append((o, o + wdt)); o += wdt
    wi = w_in[i]
    col = lambda j: wi[:, cuts[j][0]:cuts[j][1]]
    tail_pad = jnp.zeros((D, LANES - QK_ROPE - 2 * MLSTM_HEADS), F32)
    w_in_p = jnp.concatenate([col(0), col(1), col(2), col(3), col(4), col(6), col(7), col(8), col(9),
                              col(5), col(10), col(11), tail_pad], -1)
    wq = w_uq[i].reshape(Q_RANK, MLA_HEADS, QK_NOPE + QK_ROPE)
    w_uq_p = jnp.pad(wq, ((0, 0), (0, 0), (0, LANES - QK_NOPE - QK_ROPE))).reshape(Q_RANK, MLA_HEADS * LANES)
    wkv = w_ukv[i].reshape(KV_RANK, MLA_HEADS, QK_NOPE + V_DIM)
    w_uk, w_uv = wkv[..., :QK_NOPE], wkv[..., QK_NOPE:]
    w_uk_p = jnp.pad(w_uk, ((0, 0), (0, 0), (0, LANES - QK_NOPE))).reshape(KV_RANK, MLA_HEADS * LANES).astype(BF16)
    gate_b = jnp.concatenate([jnp.zeros((QK_ROPE,), F32), mlstm_gate_b[i],
                              jnp.zeros((LANES - QK_ROPE - 2 * MLSTM_HEADS,), F32)])[None, :]
    w_router = jnp.concatenate([w_group[i], w_expert[i], jnp.zeros((D, LANES - N_GROUPS - N_EXPERTS), F32)], -1)
    w_router_hi = w_router.astype(BF16)
    w_router = jnp.concatenate([w_router_hi, (w_router - w_router_hi.astype(F32)).astype(BF16)], -1)
    b_router = jnp.concatenate([b_group[i], b_expert[i], jnp.zeros((LANES - N_GROUPS - N_EXPERTS,), F32)])[None, :]
    e_gate_b, e_up_b, e_down_b = e_bf16
    lw = dict(
        layer=i, w_in=w_in_p.astype(BF16), conv_w=conv_w[i], q_norm_g=q_norm_g[i][None, :], w_uq=w_uq_p.astype(BF16),
        kv_norm_g=kv_norm_g[i][None, :], w_uk=w_uk_p,
        w_uv=jnp.pad(w_uv, ((0, 0), (0, 0), (0, LANES - V_DIM))).reshape(KV_RANK, MLA_HEADS * LANES).astype(BF16),
        w_ukT_h=jnp.transpose(w_uk, (1, 2, 0)).astype(BF16),
        w_uv_h=jnp.transpose(w_uv, (1, 0, 2)).astype(BF16),
        gate_b=gate_b, mix_norm_g=mix_norm_g[i][None, :], w_out=w_out[i].astype(BF16),
        ln1_g=ln1_g[i][None, :], ln1_b=ln1_b[i][None, :], w_router=w_router, b_router=b_router,
        e_gate=e_gate_b, e_up=e_up_b, e_down=e_down_b,
        w_pe=w_pe[i].astype(BF16), w_pg=w_pg[i].astype(BF16), ln2_g=ln2_g[i][None, :], ln2_b=ln2_b[i][None, :])
    lw_s = dict(lw, w_in=w_in_p, w_uq=w_uq_p, w_out=w_out[i], e_gate=e_gate, e_up=e_up, e_down=e_down,
                w_pe=w_pe[i], w_pg=w_pg[i])
    return lw, lw_s


def kernel(x_prompt, x_sample, cache_ckv, cache_krope, state_conv, state_mlstm_C, state_mlstm_n, state_mlstm_m,
           page_table, p_prompt, p_sample, ln0_g, ln0_b, w_in, conv_w, q_norm_g, w_uq, kv_norm_g, w_ukv,
           mlstm_gate_b, mix_norm_g, w_out, ln1_g, ln1_b, w_group, b_group, w_expert, b_expert,
           e_gate, e_up, e_down, w_pe, w_pg, ln2_g, ln2_b):
    B, S, D = x_prompt.shape
    DB, DS, _ = x_sample.shape
    assert DS == 1
    depth = w_in.shape[0]
    alpha = (2 * depth) ** 0.25
    past_len = page_table.shape[1] * cache_ckv.shape[2]
    ln0 = (ln0_g[None, :], ln0_b[None, :])
    tabs_p = _rope_tables(jnp.arange(S), S)
    tabs_s = _rope_tables(jnp.full((1,), past_len), DB)
    cache_krope_t = jnp.swapaxes(cache_krope, 2, 3)
    e_bf16 = (e_gate.astype(BF16), e_up.astype(BF16), e_down.astype(BF16))

    hp = x_prompt
    hs = x_sample.reshape(1, DB, D)
    st_p, st_s = [], []
    for i in range(depth):
        lw, lw_s = _prep_layer(i, w_in, conv_w, q_norm_g, w_uq, kv_norm_g, w_ukv, mlstm_gate_b, mix_norm_g, w_out,
                               ln1_g, ln1_b, w_group, b_group, w_expert, b_expert, e_gate, e_up, e_down, w_pe, w_pg,
                               ln2_g, ln2_b, e_bf16)
        first = i == 0
        outs = _in_proj(hp, lw, tabs_p, apply_ln0=first, ln0=ln0, seq_mode=True, emit_kv=True)
        if first:
            hp, outs = outs[0], outs[1:]
        yconv, q, ckv, krope, mqT, mk, mvT, og, gates, gatesT, k, v, conv_state = outs
        att = _flash_attn(q, k, v)
        hm, C1, n1, m1 = _mlstm_seq(mqT, mk, mvT, gates, gatesT)
        T = B * S
        flat = lambda a: a.reshape(T, a.shape[-1])
        hp = _channel_mixer(flat(yconv), flat(att), flat(hm), flat(og), flat(hp), p_prompt.reshape(depth, T, -1), lw, alpha,
                            PROMPT_EXPERT_BLOCK).reshape(B, S, D)
        st_p.append((ckv, krope, conv_state, C1, n1, m1[:, :MLSTM_HEADS, 0]))
        hist = (state_conv[i][None, :, 0, :], state_conv[i][None, :, 1, :])
        outs = _in_proj(hs, lw_s, tabs_s, apply_ln0=first, ln0=ln0, seq_mode=False, emit_kv=False, hist=hist)
        if first:
            hs, outs = outs[0], outs[1:]
        yconv, q, ckv, krope, mqT, mk, mvT, og, gates, gatesT, u = outs
        r3 = lambda a: a.reshape(DB, 1, a.shape[-1])
        att = _paged_attn(r3(q), r3(ckv), r3(krope), lw["w_ukT_h"], lw["w_uv_h"], cache_ckv, cache_krope_t,
                          page_table, i)
        hm, C1, n1, m1 = _mlstm_step(jnp.swapaxes(mqT[0], 0, 1), mk[0], jnp.swapaxes(mvT[0], 0, 1), gates[0],
                                     state_mlstm_C[i], state_mlstm_n[i], state_mlstm_m[i])
        hs = _channel_mixer(yconv[0], att.reshape(DB, -1), hm.reshape(DB, -1), og[0], hs[0], p_sample.reshape(depth, DB, -1),
                            lw_s, alpha, SAMPLE_EXPERT_BLOCK).reshape(1, DB, D)
        conv_new = jnp.stack([state_conv[i][:, 1, :], u[0]], axis=1)
        st_s.append((ckv.reshape(DB, 1, -1), krope.reshape(DB, 1, -1), conv_new, C1, n1, m1.reshape(DB, MLSTM_HEADS)))
    ckv_p, krope_p, conv_p, C_p, n_p, m_p = [jnp.stack(a) for a in zip(*st_p)]
    ckv_s, krope_s, conv_s, C_s, n_s, m_s = [jnp.stack(a) for a in zip(*st_s)]
    return (hp, hs.reshape(DB, DS, D), ckv_p, krope_p, conv_p, C_p, n_p, m_p,
            ckv_s, krope_s, conv_s, C_s, n_s, m_s)
```
